```python
import math
import jax, jax.numpy as jnp
from jax import lax
import numpy as np

D_MODEL = 2048
BATCH = 2
SEQ = 4096
DEPTH = 4
DEC_BATCH = 8
DEC_SEQ = 4
PAST_LEN = 16384
PAGE_SIZE = 128

N_MIXERS = 2
N_POOL_LAYERS = (DEPTH + 1) // 2
N_ATTN_LAYERS = DEPTH // 2
POOL_WINDOWS = (2, 4, 8, 16)
POOL_GROUP = D_MODEL // len(POOL_WINDOWS)
POOL_STATE = max(POOL_WINDOWS) - 1
HEAD_DIM = 128
V_DIM = 2 * HEAD_DIM
N_HEADS = D_MODEL // V_DIM
QK_WIDTH = N_HEADS * 2 * HEAD_DIM
ATTN_WIDTH = N_HEADS * V_DIM
QKV_WIDTH = 2 * QK_WIDTH + ATTN_WIDTH
ROPE_THETA = 10000.0
Q_BLOCK = 128
N_GROUPS = 4
EXPERTS_PER_GROUP = 4
N_EXPERTS = N_GROUPS * EXPERTS_PER_GROUP
TOP_K_INNER = 2
D_EXPERT = D_MODEL // 4
EPS = 1e-6
NEG = -1e30

kernel_name = 'hybrid_pool_diffattn_hmoe_step'


def rms_norm(x, g):
    xf = x.astype(jnp.float32)
    y = xf * lax.rsqrt(jnp.mean(xf * xf, axis=-1, keepdims=True) + EPS)
    return (y * g.astype(jnp.float32)).astype(x.dtype)


def ada_modulate(x, g, shift, scale):
    return rms_norm(x, g) * (1 + scale[:, None]) + shift[:, None]


def rope(x, pos):
    half = HEAD_DIM // 2
    inv = ROPE_THETA ** (-jnp.arange(half, dtype=jnp.float32) / half)
    ang = pos.astype(jnp.float32)[:, None] * inv[None, :]
    cos = jnp.cos(ang)[:, None, None, :]
    sin = jnp.sin(ang)[:, None, None, :]
    xf = x.astype(jnp.float32)
    x1, x2 = xf[..., :half], xf[..., half:]
    return jnp.concatenate([x1 * cos - x2 * sin, x2 * cos + x1 * sin], axis=-1).astype(x.dtype)


def pool_mixer(h, pos, w_pool, b_pool, ls):
    outs = []
    for gi, w in enumerate(POOL_WINDOWS):
        hg = h[..., gi * POOL_GROUP:(gi + 1) * POOL_GROUP].astype(jnp.float32)
        cs = jnp.cumsum(hg, axis=1)
        prev = jnp.pad(cs[:, :-w], ((0, 0), (w, 0), (0, 0)))
        cnt = jnp.minimum(pos + 1, w).astype(jnp.float32)[None, :, None]
        d = ((cs - prev) / cnt - hg).astype(h.dtype)
        outs.append(jnp.einsum('blc,cd->bld', d, w_pool[gi]) + b_pool[gi])
    return jnp.concatenate(outs, axis=-1) * ls


def lambda_init(layer):
    return 0.8 - 0.6 * math.exp(-0.3 * layer)


def diff_lambda(lq1, lk1, lq2, lk2, layer):
    f = jnp.float32
    return (jnp.exp(jnp.sum(lq1.astype(f) * lk1.astype(f)))
            - jnp.exp(jnp.sum(lq2.astype(f) * lk2.astype(f))) + lambda_init(layer))


def diff_qkv(h, w_qkv, g_q, g_k, pos):
    b, s, _ = h.shape
    qkv = h @ w_qkv
    q = qkv[..., :QK_WIDTH].reshape(b, s, N_HEADS, 2, HEAD_DIM)
    k = qkv[..., QK_WIDTH:2 * QK_WIDTH].reshape(b, s, N_HEADS, 2, HEAD_DIM)
    v = qkv[..., 2 * QK_WIDTH:].reshape(b, s, N_HEADS, V_DIM)
    q = rope(rms_norm(q, g_q), pos)
    k = rope(rms_norm(k, g_k), pos)
    return q, k, v


def prompt_diff_attn(q, k, v, lam):
    b, s = q.shape[:2]
    nb = s // Q_BLOCK
    qb = q.reshape(b, nb, Q_BLOCK, N_HEADS, 2, HEAD_DIM).transpose(1, 0, 2, 3, 4, 5)
    kpos = jnp.arange(s)
    scale = HEAD_DIM ** -0.5

    def block(args):
        qi, bi = args
        sc = jnp.einsum('bqhjd,bkhjd->bhjqk', qi, k, preferred_element_type=jnp.float32) * scale
        qpos = bi * Q_BLOCK + jnp.arange(Q_BLOCK)
        sc = jnp.where(kpos[None, :] <= qpos[:, None], sc, NEG)
        p = jax.nn.softmax(sc, axis=-1)
        a = p[:, :, 0] - lam * p[:, :, 1]
        return jnp.einsum('bhqk,bkhv->bqhv', a, v, preferred_element_type=jnp.float32)

    out = lax.map(block, (qb, jnp.arange(nb)))
    return out.transpose(1, 0, 2, 3, 4).reshape(b, s, N_HEADS, V_DIM).astype(v.dtype)


def sample_diff_attn(q, k_new, v_new, k_past, v_past, lam):
    scale = HEAD_DIM ** -0.5
    t = q.shape[1]
    p_len = k_past.shape[1]
    s_past = jnp.einsum('bqhjd,bkhjd->bhjqk', q, k_past, preferred_element_type=jnp.float32) * scale
    s_new = jnp.einsum('bqhjd,bkhjd->bhjqk', q, k_new, preferred_element_type=jnp.float32) * scale
    s_new = jnp.where(jnp.tril(jnp.ones((t, t), dtype=bool)), s_new, NEG)
    p = jax.nn.softmax(jnp.concatenate([s_past, s_new], axis=-1), axis=-1)
    a = p[:, :, 0] - lam * p[:, :, 1]
    out = (jnp.einsum('bhqk,bkhv->bqhv', a[..., :p_len], v_past, preferred_element_type=jnp.float32)
           + jnp.einsum('bhqk,bkhv->bqhv', a[..., p_len:], v_new, preferred_element_type=jnp.float32))
    return out.astype(v_new.dtype)


def diff_out(o, g_sub, w_o, layer):
    b, s = o.shape[:2]
    o = rms_norm(o, g_sub) * (1 - lambda_init(layer))
    return o.reshape(b, s, ATTN_WIDTH) @ w_o


def hier_moe(h, w_rg, b_rg, w_re, b_re, w1, w3, w2):
    b, s, d = h.shape
    t = h.reshape(-1, d)
    pg = jax.nn.softmax((t @ w_rg).astype(jnp.float32) + b_rg, axis=-1)
    pg_top, g_top = lax.top_k(pg, 1)
    le = ((t @ w_re).astype(jnp.float32) + b_re).reshape(-1, N_GROUPS, EXPERTS_PER_GROUP)
    le_sel = jnp.take_along_axis(le, g_top[:, :, None], axis=1)[:, 0]
    w_top, e_top = lax.top_k(le_sel, TOP_K_INNER)
    w_top = jax.nn.softmax(w_top, axis=-1)
    gate = pg_top * w_top
    idx = g_top * EXPERTS_PER_GROUP + e_top
    gate_full = jnp.sum(jax.nn.one_hot(idx, N_EXPERTS, dtype=jnp.float32) * gate[..., None], axis=1)
    a = jnp.einsum('td,edf->tef', t, w1)
    c = jnp.einsum('td,edf->tef', t, w3)
    act = jax.nn.silu(a) * c * gate_full[..., None].astype(t.dtype)
    return jnp.einsum('tef,efd->td', act, w2).reshape(b, s, d)


def setup_inputs(seed: int = 0) -> dict:
    key = jax.random.key(seed)
    ks = iter(jax.random.split(key, 40))

    def nrm(shape, s):
        return jax.random.normal(next(ks), shape, jnp.float32) * s

    n_pages = PAST_LEN // PAGE_SIZE
    n_used = DEC_BATCH * n_pages
    n_pool_pages = n_used + max(1, n_used // 4)
    perm = jax.random.permutation(next(ks), n_pool_pages)
    page_table = perm[:n_used].reshape(DEC_BATCH, n_pages).astype(jnp.int32)
    d = D_MODEL
    return {
        'x_prompt': nrm((BATCH, SEQ, d), 1.0),
        'x_sample': nrm((DEC_BATCH, DEC_SEQ, d), 1.0),
        'state_pool': nrm((N_POOL_LAYERS, DEC_BATCH, POOL_STATE, d), 1.0),
        'cache_k': nrm((N_ATTN_LAYERS, n_pool_pages, PAGE_SIZE, N_HEADS, 2, HEAD_DIM), 1.0),
        'cache_v': nrm((N_ATTN_LAYERS, n_pool_pages, PAGE_SIZE, N_HEADS, V_DIM), 1.0),
        'page_table': page_table,
        'c_prompt': nrm((BATCH, d), 1.0),
        'c_sample': nrm((DEC_BATCH, d), 1.0),
        'w_ada': nrm((DEPTH, d, 6 * d), 0.5 * d ** -0.5),
        'b_ada': nrm((DEPTH, 6 * d), 0.1),
        'g_norm1': 1.0 + nrm((DEPTH, d), 0.1),
        'g_norm2': 1.0 + nrm((DEPTH, d), 0.1),
        'w_pool': nrm((N_POOL_LAYERS, len(POOL_WINDOWS), POOL_GROUP, POOL_GROUP), POOL_GROUP ** -0.5),
        'b_pool': nrm((N_POOL_LAYERS, len(POOL_WINDOWS), POOL_GROUP), 0.02),
        'pool_scale': 1.0 + nrm((N_POOL_LAYERS, d), 0.1),
        'w_qkv': nrm((N_ATTN_LAYERS, d, QKV_WIDTH), d ** -0.5),
        'g_q': 1.0 + nrm((N_ATTN_LAYERS, HEAD_DIM), 0.1),
        'g_k': 1.0 + nrm((N_ATTN_LAYERS, HEAD_DIM), 0.1),
        'lam_q1': nrm((N_ATTN_LAYERS, HEAD_DIM), 0.1),
        'lam_k1': nrm((N_ATTN_LAYERS, HEAD_DIM), 0.1),
        'lam_q2': nrm((N_ATTN_LAYERS, HEAD_DIM), 0.1),
        'lam_k2': nrm((N_ATTN_LAYERS, HEAD_DIM), 0.1),
        'g_sub': 1.0 + nrm((N_ATTN_LAYERS, V_DIM), 0.1),
        'w_o': nrm((N_ATTN_LAYERS, ATTN_WIDTH, d), ATTN_WIDTH ** -0.5),
        'w_rg': nrm((DEPTH, d, N_GROUPS), d ** -0.5),
        'b_rg': nrm((DEPTH, N_GROUPS), 0.01),
        'w_re': nrm((DEPTH, d, N_EXPERTS), d ** -0.5),
        'b_re': nrm((DEPTH, N_EXPERTS), 0.01),
        'w1': nrm((DEPTH, N_EXPERTS, d, D_EXPERT), d ** -0.5),
        'w3': nrm((DEPTH, N_EXPERTS, d, D_EXPERT), d ** -0.5),
        'w2': nrm((DEPTH, N_EXPERTS, D_EXPERT, d), D_EXPERT ** -0.5),
    }


def reference(x_prompt, x_sample, state_pool, cache_k, cache_v, page_table, c_prompt, c_sample,
              w_ada, b_ada, g_norm1, g_norm2, w_pool, b_pool, pool_scale, w_qkv, g_q, g_k,
              lam_q1, lam_k1, lam_q2, lam_k2, g_sub, w_o, w_rg, b_rg, w_re, b_re, w1, w3, w2):
    seq = x_prompt.shape[1]
    dec_b, dec_t = x_sample.shape[:2]
    past_len = page_table.shape[1] * cache_k.shape[2]
    pos_p = jnp.arange(seq)
    pos_s = past_len + jnp.arange(dec_t)
    pos_ext = past_len - POOL_STATE + jnp.arange(POOL_STATE + dec_t)
    xp, xs = x_prompt, x_sample
    nk_p, nv_p, npool_p, nk_s, nv_s, npool_s = [], [], [], [], [], []
    for i in range(DEPTH):
        mp = jnp.split(jax.nn.silu(c_prompt) @ w_ada[i] + b_ada[i], 6, axis=-1)
        ms = jnp.split(jax.nn.silu(c_sample) @ w_ada[i] + b_ada[i], 6, axis=-1)
        hp = ada_modulate(xp, g_norm1[i], mp[0], mp[1])
        hs = ada_modulate(xs, g_norm1[i], ms[0], ms[1])
        li = i // N_MIXERS
        if i % N_MIXERS == 0:
            yp = pool_mixer(hp, pos_p, w_pool[li], b_pool[li], pool_scale[li])
            h_ext = jnp.concatenate([state_pool[li], hs], axis=1)
            ys = pool_mixer(h_ext, pos_ext, w_pool[li], b_pool[li], pool_scale[li])[:, POOL_STATE:]
            npool_p.append(hp[:, seq - POOL_STATE:])
            npool_s.append(h_ext[:, dec_t:])
        else:
            lam = diff_lambda(lam_q1[li], lam_k1[li], lam_q2[li], lam_k2[li], i)
            qp, kp, vp = diff_qkv(hp, w_qkv[li], g_q[li], g_k[li], pos_p)
            yp = diff_out(prompt_diff_attn(qp, kp, vp, lam), g_sub[li], w_o[li], i)
            qs, ks_new, vs_new = diff_qkv(hs, w_qkv[li], g_q[li], g_k[li], pos_s)
            k_past = cache_k[li, page_table].reshape(dec_b, past_len, N_HEADS, 2, HEAD_DIM)
            v_past = cache_v[li, page_table].reshape(dec_b, past_len, N_HEADS, V_DIM)
            ys = diff_out(sample_diff_attn(qs, ks_new, vs_new, k_past, v_past, lam), g_sub[li], w_o[li], i)
            nk_p.append(kp)
            nv_p.append(vp)
            nk_s.append(ks_new)
            nv_s.append(vs_new)
        xp = xp + mp[2][:, None] * yp
        xs = xs + ms[2][:, None] * ys
        hp = ada_modulate(xp, g_norm2[i], mp[3], mp[4])
        hs = ada_modulate(xs, g_norm2[i], ms[3], ms[4])
        xp = xp + mp[5][:, None] * hier_moe(hp, w_rg[i], b_rg[i], w_re[i], b_re[i], w1[i], w3[i], w2[i])
        xs = xs + ms[5][:, None] * hier_moe(hs, w_rg[i], b_rg[i], w_re[i], b_re[i], w1[i], w3[i], w2[i])
    return (xp, xs, jnp.stack(nk_p), jnp.stack(nv_p), jnp.stack(npool_p),
            jnp.stack(nk_s), jnp.stack(nv_s), jnp.stack(npool_s))
```

```python
import functools
import math

import jax
import jax.numpy as jnp
import numpy as np
from jax import lax
from jax.experimental import pallas as pl
from jax.experimental.pallas import tpu as pltpu

F32 = jnp.float32
BF16 = jnp.bfloat16

EPS = 1e-6
NEG = -1e30
ROPE_THETA = 10000.0
POOL_WINDOWS = (2, 4, 8, 16)
HIST = 16
HEAD_DIM = 128
V_DIM = 2 * HEAD_DIM
N_GROUPS = 4
EXPERTS_PER_GROUP = 4
N_PAIRS = 6
N_BUCKETS = N_GROUPS * N_PAIRS
LANES = 128
MXU_ROWS = 256
VMEM_LIMIT = 48 * 1024 * 1024

_PAIR_LO = (0, 0, 0, 1, 1, 2)
_PAIR_HI = (1, 2, 3, 2, 3, 3)


def _lambda_init(layer):
    return 0.8 - 0.6 * math.exp(-0.3 * layer)


def _tile(n, pref):
    if n <= pref:
        return n
    t = pref
    while t >= 8:
        if n % t == 0 and t % 8 == 0:
            return t
        t -= 8
    return n


def _params(*sem):
    return pltpu.CompilerParams(dimension_semantics=sem, vmem_limit_bytes=VMEM_LIMIT)


def _modulate(x, g, shift, scale):
    ms = jnp.mean(x * x, axis=-1, keepdims=True)
    y = x * lax.rsqrt(ms + EPS) * g
    return y * (1.0 + scale) + shift


def _ada_kernel(c_ref, w_ref, b_ref, o_ref):
    c = c_ref[...]
    s = (c * jax.nn.sigmoid(c)).astype(BF16)
    o_ref[0] = jnp.dot(s, w_ref[0].astype(BF16), preferred_element_type=F32) + b_ref[0]


def _ada(c_all, w_ada, b_ada):
    depth, d, n = w_ada.shape
    r = c_all.shape[0]
    tn = _tile(n, 1024)
    return pl.pallas_call(
        _ada_kernel,
        grid=(depth, n // tn),
        in_specs=[pl.BlockSpec((r, d), lambda l, j: (0, 0)),
                  pl.BlockSpec((1, d, tn), lambda l, j: (l, 0, j)),
                  pl.BlockSpec((1, 1, tn), lambda l, j: (l, 0, j))],
        out_specs=pl.BlockSpec((1, r, tn), lambda l, j: (l, 0, j)),
        out_shape=jax.ShapeDtypeStruct((depth, r, n), F32),
        compiler_params=_params("arbitrary", "arbitrary"),
        name="ada",
    )(c_all, w_ada, b_ada.reshape(depth, 1, n))


def _pool_kernel(x_ref, hist_ref, shift_ref, scale_ref, gate_ref, g_ref, w_ref, b_ref, ls_ref,
                 xo_ref, tail_ref, hbuf, *, tm, pos0, n_tiles):
    j = pl.program_id(1)
    d = x_ref.shape[-1]
    grp = d // len(POOL_WINDOWS)

    @pl.when(j == 0)
    def _():
        hbuf[0:HIST, :] = hist_ref[0]

    x = x_ref[0]
    h = _modulate(x, g_ref[...], shift_ref[0], scale_ref[0])
    hbuf[HIST:HIST + tm, :] = h
    pos = pos0 + j * tm + lax.broadcasted_iota(jnp.int32, (tm, 1), 0)
    gate = gate_ref[0]
    for gi, w in enumerate(POOL_WINDOWS):
        lo = gi * grp
        s = hbuf[:, lo:lo + grp]
        k = 1
        while k < w:
            s = s + pltpu.roll(s, k, 0)
            k *= 2
        cnt = jnp.minimum(pos + 1, w).astype(F32)
        dlt = s[HIST:, :] / cnt - h[:, lo:lo + grp]
        y = jnp.dot(dlt.astype(BF16), w_ref[gi], preferred_element_type=F32) + b_ref[:, lo:lo + grp]
        y = y * ls_ref[:, lo:lo + grp]
        xo_ref[0, :, lo:lo + grp] = x[:, lo:lo + grp] + gate[:, lo:lo + grp] * y

    @pl.when(j == n_tiles - 1)
    def _():
        tail_ref[0] = hbuf[tm:tm + HIST, :]

    if n_tiles > 1:
        @pl.when(j < n_tiles - 1)
        def _():
            hbuf[0:HIST, :] = hbuf[tm:tm + HIST, :]


def _pool(x, hist, shift, scale, gate, g, w_pool_bf, b_pool, ls, pos0):
    b, s, d = x.shape
    tm = _tile(s, 512)
    n_tiles = s // tm
    assert n_tiles == 1 or tm >= HIST
    ng, grp, _ = w_pool_bf.shape
    row = lambda bb, j: (bb, 0, 0)
    return pl.pallas_call(
        functools.partial(_pool_kernel, tm=tm, pos0=pos0, n_tiles=n_tiles),
        grid=(b, n_tiles),
        in_specs=[pl.BlockSpec((1, tm, d), lambda bb, j: (bb, j, 0)),
                  pl.BlockSpec((1, HIST, d), row),
                  pl.BlockSpec((1, 1, d), row), pl.BlockSpec((1, 1, d), row), pl.BlockSpec((1, 1, d), row),
                  pl.BlockSpec((1, d), lambda bb, j: (0, 0)),
                  pl.BlockSpec((ng, grp, grp), lambda bb, j: (0, 0, 0)),
                  pl.BlockSpec((1, d), lambda bb, j: (0, 0)),
                  pl.BlockSpec((1, d), lambda bb, j: (0, 0))],
        out_specs=[pl.BlockSpec((1, tm, d), lambda bb, j: (bb, j, 0)),
                   pl.BlockSpec((1, HIST, d), row)],
        out_shape=[jax.ShapeDtypeStruct((b, s, d), F32), jax.ShapeDtypeStruct((b, HIST, d), F32)],
        scratch_shapes=[pltpu.VMEM((HIST + tm, d), F32)],
        compiler_params=_params("arbitrary", "arbitrary"),
        name="pool_mixer",
    )(x, hist, shift, scale, gate, g, w_pool_bf, b_pool, ls)


def _qkv_kernel(x_ref, shift_ref, scale_ref, g_ref, w_ref, gq_ref, gk_ref, cos_ref, sin_ref,
                q_ref, k_ref, kb_ref, v_ref, vb_ref, h_scr, *, nq, nk, q_scale):
    n = pl.program_id(2)
    tn = w_ref.shape[1]

    @pl.when(n == 0)
    def _():
        h_scr[...] = _modulate(x_ref[0], g_ref[...], shift_ref[0], scale_ref[0]).astype(BF16)

    acc = jnp.dot(h_scr[...], w_ref[...], preferred_element_type=F32)

    def norm_rope(c, g):
        xc = acc[:, c * HEAD_DIM:(c + 1) * HEAD_DIM]
        y = xc * lax.rsqrt(jnp.mean(xc * xc, axis=-1, keepdims=True) + EPS) * g
        return y * cos_ref[...] + pltpu.roll(y, HEAD_DIM // 2, 1) * sin_ref[...]

    @pl.when(n < nq)
    def _():
        for c in range(tn // HEAD_DIM):
            q_ref[0, :, c * HEAD_DIM:(c + 1) * HEAD_DIM] = (norm_rope(c, gq_ref[...]) * q_scale).astype(BF16)

    @pl.when((n >= nq) & (n < nq + nk))
    def _():
        for c in range(tn // HEAD_DIM):
            y = norm_rope(c, gk_ref[...])
            k_ref[0, :, c * HEAD_DIM:(c + 1) * HEAD_DIM] = y
            kb_ref[0, :, c * HEAD_DIM:(c + 1) * HEAD_DIM] = y.astype(BF16)

    @pl.when(n >= nq + nk)
    def _():
        v_ref[0] = acc
        vb_ref[0] = acc.astype(BF16)


def _qkv(x, shift, scale, g, w_qkv_bf, g_q, g_k, cos, sin, qk_width):
    b, s, d = x.shape
    n_all = w_qkv_bf.shape[1]
    v_width = n_all - 2 * qk_width
    tm = _tile(s, 1024)
    tn = 512 if (qk_width % 512 == 0 and v_width % 512 == 0) else HEAD_DIM * 2
    nq = nk = qk_width // tn
    nv = v_width // tn
    r = shift.shape[1]
    rm = tm if r > 1 else 1
    mod_spec = pl.BlockSpec((1, rm, d), (lambda bb, i, n: (bb, i, 0)) if r > 1 else (lambda bb, i, n: (bb, 0, 0)))
    qmap = lambda bb, i, n: (bb, i, jnp.minimum(n, nq - 1))
    kmap = lambda bb, i, n: (bb, i, jnp.clip(n - nq, 0, nk - 1))
    vmap = lambda bb, i, n: (bb, i, jnp.clip(n - nq - nk, 0, nv - 1))
    tab_spec = pl.BlockSpec((tm, HEAD_DIM), lambda bb, i, n: (i, 0))
    vec_spec = pl.BlockSpec((1, HEAD_DIM), lambda bb, i, n: (0, 0))
    return pl.pallas_call(
        functools.partial(_qkv_kernel, nq=nq, nk=nk, q_scale=HEAD_DIM ** -0.5),
        grid=(b, s // tm, nq + nk + nv),
        in_specs=[pl.BlockSpec((1, tm, d), lambda bb, i, n: (bb, i, 0)),
                  mod_spec, mod_spec,
                  pl.BlockSpec((1, d), lambda bb, i, n: (0, 0)),
                  pl.BlockSpec((d, tn), lambda bb, i, n: (0, n)),
                  vec_spec, vec_spec, tab_spec, tab_spec],
        out_specs=[pl.BlockSpec((1, tm, tn), qmap),
                   pl.BlockSpec((1, tm, tn), kmap), pl.BlockSpec((1, tm, tn), kmap),
                   pl.BlockSpec((1, tm, tn), vmap), pl.BlockSpec((1, tm, tn), vmap)],
        out_shape=[jax.ShapeDtypeStruct((b, s, qk_width), BF16),
                   jax.ShapeDtypeStruct((b, s, qk_width), F32), jax.ShapeDtypeStruct((b, s, qk_width), BF16),
                   jax.ShapeDtypeStruct((b, s, v_width), F32), jax.ShapeDtypeStruct((b, s, v_width), BF16)],
        scratch_shapes=[pltpu.VMEM((tm, d), BF16)],
        compiler_params=_params("arbitrary", "arbitrary", "arbitrary"),
        name="qkv_proj",
    )(x, shift, scale, g, w_qkv_bf, g_q, g_k, cos, sin)


def _lam_value(lam_ref, lam_init):
    l = lam_ref[...]
    a = jnp.sum(l[0:1] * l[1:2], axis=-1, keepdims=True)
    b = jnp.sum(l[2:3] * l[3:4], axis=-1, keepdims=True)
    return jnp.exp(a) - jnp.exp(b) + lam_init


def _softmax_step(s, v_bf, m_prev, l_prev, acc_prev):
    m_new = jnp.maximum(m_prev, jnp.max(s, axis=-1, keepdims=True))
    alpha = jnp.exp(m_prev - m_new)
    p = jnp.exp(s - m_new)
    l_new = alpha * l_prev + jnp.sum(p, axis=-1, keepdims=True)
    acc_new = alpha * acc_prev + jnp.dot(p.astype(BF16), v_bf, preferred_element_type=F32)
    return m_new, l_new, acc_new


def _head_out(o0, o1, lam, gsub, out_scale):
    o = o0 - lam * o1
    return o * lax.rsqrt(jnp.mean(o * o, axis=-1, keepdims=True) + EPS) * gsub * out_scale


def _attn_kernel(qi_ref, kj_ref, q_ref, k_ref, v_ref, lam_ref, gsub_ref, o_ref, m_scr, l_scr, acc_scr,
                 *, tq, lam_init, out_scale):
    t = pl.program_id(2)
    i = qi_ref[t]
    j = kj_ref[t]

    @pl.when(j == 0)
    def _():
        m_scr[...] = jnp.full(m_scr.shape, NEG, F32)
        l_scr[...] = jnp.zeros(l_scr.shape, F32)
        acc_scr[...] = jnp.zeros(acc_scr.shape, F32)

    q = q_ref[0]
    k = k_ref[0]
    nt = (((1,), (1,)), ((), ()))
    s0 = lax.dot_general(q[:, :HEAD_DIM], k[:, :HEAD_DIM], nt, preferred_element_type=F32)
    s1 = lax.dot_general(q[:, HEAD_DIM:], k[:, HEAD_DIM:], nt, preferred_element_type=F32)
    s = jnp.concatenate([s0, s1], axis=0)

    def step(sc):
        m, l, acc = _softmax_step(sc, v_ref[0], m_scr[...], l_scr[...], acc_scr[...])
        m_scr[...] = m
        l_scr[...] = l
        acc_scr[...] = acc

    @pl.when(j < i)
    def _():
        step(s)

    @pl.when(j == i)
    def _():
        row = lax.broadcasted_iota(jnp.int32, s.shape, 0)
        row = jnp.where(row >= tq, row - tq, row)
        col = lax.broadcasted_iota(jnp.int32, s.shape, 1)
        step(jnp.where(col <= row, s, NEG))
        acc = acc_scr[...]
        l = l_scr[...]
        lam = _lam_value(lam_ref, lam_init)
        o = _head_out(acc[:tq] / l[:tq], acc[tq:] / l[tq:], lam, gsub_ref[...], out_scale)
        o_ref[0] = o.astype(BF16)


def _prompt_attn(q_bf, k_bf, v_bf, lam_vecs, g_sub, layer):
    b, s, qk = q_bf.shape
    h = qk // (2 * HEAD_DIM)
    tq = _tile(s, 512)
    nb = s // tq
    qi = np.array([i for i in range(nb) for _ in range(i + 1)], np.int32)
    kj = np.array([j for i in range(nb) for j in range(i + 1)], np.int32)
    grid_spec = pltpu.PrefetchScalarGridSpec(
        num_scalar_prefetch=2,
        grid=(b, h, len(qi)),
        in_specs=[pl.BlockSpec((1, tq, 2 * HEAD_DIM), lambda bb, hh, t, qi_r, kj_r: (bb, qi_r[t], hh)),
                  pl.BlockSpec((1, tq, 2 * HEAD_DIM), lambda bb, hh, t, qi_r, kj_r: (bb, kj_r[t], hh)),
                  pl.BlockSpec((1, tq, V_DIM), lambda bb, hh, t, qi_r, kj_r: (bb, kj_r[t], hh)),
                  pl.BlockSpec((4, HEAD_DIM), lambda bb, hh, t, qi_r, kj_r: (0, 0)),
                  pl.BlockSpec((1, V_DIM), lambda bb, hh, t, qi_r, kj_r: (0, 0))],
        out_specs=pl.BlockSpec((1, tq, V_DIM), lambda bb, hh, t, qi_r, kj_r: (bb, qi_r[t], hh)),
        scratch_shapes=[pltpu.VMEM((2 * tq, 1), F32), pltpu.VMEM((2 * tq, 1), F32),
                        pltpu.VMEM((2 * tq, V_DIM), F32)],
    )
    return pl.pallas_call(
        functools.partial(_attn_kernel, tq=tq, lam_init=_lambda_init(layer), out_scale=1.0 - _lambda_init(layer)),
        grid_spec=grid_spec,
        out_shape=jax.ShapeDtypeStruct((b, s, h * V_DIM), BF16),
        compiler_params=_params("arbitrary", "arbitrary", "arbitrary"),
        name="prompt_attn",
    )(jnp.asarray(qi), jnp.asarray(kj), q_bf, k_bf, v_bf, lam_vecs, g_sub)


def _decode_kernel(pt_ref, q_ref, kn_ref, vn_ref, kc_ref, vc_ref, lam_ref, gsub_ref, o_ref,
                   qf_scr, qbd_scr, m_scr, l_scr, acc_scr, *, n_heads, t_new, lam_init, out_scale):
    j = pl.program_id(1)
    n_pages = pl.num_programs(1)
    rows = q_ref.shape[1]
    hr = 2 * rows
    nt = (((1,), (1,)), ((), ()))

    @pl.when(j == 0)
    def _():
        qf_scr[...] = jnp.zeros(qf_scr.shape, F32)
        q = q_ref[0]
        for hm in range(2 * n_heads):
            qf_scr[hm * rows:(hm + 1) * rows, hm * HEAD_DIM:(hm + 1) * HEAD_DIM] = q[:, hm * HEAD_DIM:(hm + 1) * HEAD_DIM]
        qbd_scr[...] = qf_scr[...].astype(BF16)
        m_scr[...] = jnp.full(m_scr.shape, NEG, F32)
        l_scr[...] = jnp.zeros(l_scr.shape, F32)
        acc_scr[...] = jnp.zeros(acc_scr.shape, F32)

    def update(s, v_of_head):
        m_prev = m_scr[...]
        m_new = jnp.maximum(m_prev, jnp.max(s, axis=-1, keepdims=True))
        alpha = jnp.exp(m_prev - m_new)
        p = jnp.exp(s - m_new)
        l_scr[...] = alpha * l_scr[...] + jnp.sum(p, axis=-1, keepdims=True)
        m_scr[...] = m_new
        p_bf = p.astype(BF16)
        for h in range(n_heads):
            pv = jnp.dot(p_bf[h * hr:(h + 1) * hr], v_of_head(h), preferred_element_type=F32)
            acc_scr[h] = alpha[h * hr:(h + 1) * hr] * acc_scr[h] + pv

    k_page = kc_ref[0, 0].astype(BF16)
    v_page = vc_ref[0, 0].astype(BF16)
    s_past = lax.dot_general(qbd_scr[...], k_page, nt, preferred_element_type=F32)
    update(s_past, lambda h: v_page[:, h * V_DIM:(h + 1) * V_DIM])

    @pl.when(j == n_pages - 1)
    def _():
        k_new = kn_ref[0]
        v_new = vn_ref[0]
        s_new = lax.dot_general(qbd_scr[...], k_new, nt, preferred_element_type=F32)
        r = jnp.bitwise_and(lax.broadcasted_iota(jnp.int32, s_new.shape, 0), rows - 1)
        c = lax.broadcasted_iota(jnp.int32, s_new.shape, 1)
        s_new = jnp.where((c <= r) & (c < t_new), s_new, NEG)
        update(s_new, lambda h: v_new[:, h * V_DIM:(h + 1) * V_DIM])
        lam = _lam_value(lam_ref, lam_init)
        l = l_scr[...]
        for h in range(n_heads):
            acc = acc_scr[h]
            lh = l[h * hr:(h + 1) * hr]
            o = _head_out(acc[:rows] / lh[:rows], acc[rows:] / lh[rows:], lam, gsub_ref[...], out_scale)
            o_ref[0, :, h * V_DIM:(h + 1) * V_DIM] = o


def _decode_attn(q, kn_bf, vn_bf, cache_k, cache_v, page_table, li, lam_vecs, g_sub, layer, t_new):
    bd, rows, qk = q.shape
    keys = kn_bf.shape[1]
    assert rows & (rows - 1) == 0 and t_new <= rows <= keys
    h = qk // (2 * HEAD_DIM)
    page = cache_k.shape[2]
    n_pages = page_table.shape[1]
    cmap = lambda bb, j, pt: (li, pt[bb * n_pages + j], 0, 0)
    seq = lambda bb, j, pt: (bb, 0, 0)
    grid_spec = pltpu.PrefetchScalarGridSpec(
        num_scalar_prefetch=1,
        grid=(bd, n_pages),
        in_specs=[pl.BlockSpec((1, rows, qk), seq), pl.BlockSpec((1, keys, qk), seq),
                  pl.BlockSpec((1, keys, h * V_DIM), seq),
                  pl.BlockSpec((1, 1, page, qk), cmap), pl.BlockSpec((1, 1, page, h * V_DIM), cmap),
                  pl.BlockSpec((4, HEAD_DIM), lambda bb, j, pt: (0, 0)),
                  pl.BlockSpec((1, V_DIM), lambda bb, j, pt: (0, 0))],
        out_specs=pl.BlockSpec((1, rows, h * V_DIM), seq),
        scratch_shapes=[pltpu.VMEM((2 * h * rows, qk), F32), pltpu.VMEM((2 * h * rows, qk), BF16),
                        pltpu.VMEM((2 * h * rows, 1), F32), pltpu.VMEM((2 * h * rows, 1), F32),
                        pltpu.VMEM((h, 2 * rows, V_DIM), F32)],
    )
    return pl.pallas_call(
        functools.partial(_decode_kernel, n_heads=h, t_new=t_new, lam_init=_lambda_init(layer),
                          out_scale=1.0 - _lambda_init(layer)),
        grid_spec=grid_spec,
        out_shape=jax.ShapeDtypeStruct((bd, rows, h * V_DIM), F32),
        compiler_params=_params("arbitrary", "arbitrary"),
        name="decode_attn",
    )(page_table.reshape(-1), q, kn_bf, vn_bf, cache_k, cache_v, lam_vecs, g_sub)


def _oproj_kernel(o_ref, w_ref, x_ref, gate_ref, xo_ref):
    y = jnp.dot(o_ref[0], w_ref[...], preferred_element_type=F32)
    xo_ref[0] = x_ref[0] + gate_ref[0] * y


def _oproj(o_bf, w_o_bf, x, gate):
    b, s, d = x.shape
    kdim = o_bf.shape[-1]
    tm = _tile(s, 512)
    r = gate.shape[1]
    rm = tm if r > 1 else 1
    gate_spec = pl.BlockSpec((1, rm, d), (lambda bb, i: (bb, i, 0)) if r > 1 else (lambda bb, i: (bb, 0, 0)))
    return pl.pallas_call(
        _oproj_kernel,
        grid=(b, s // tm),
        in_specs=[pl.BlockSpec((1, tm, kdim), lambda bb, i: (bb, i, 0)),
                  pl.BlockSpec((kdim, d), lambda bb, i: (0, 0)),
                  pl.BlockSpec((1, tm, d), lambda bb, i: (bb, i, 0)),
                  gate_spec],
        out_specs=pl.BlockSpec((1, tm, d), lambda bb, i: (bb, i, 0)),
        out_shape=jax.ShapeDtypeStruct((b, s, d), F32),
        compiler_params=_params("arbitrary", "arbitrary"),
        name="out_proj",
    )(o_bf, w_o_bf, x, gate)


def _split_bf16(a):
    hi = a.astype(BF16)
    lo = (a - hi.astype(F32)).astype(BF16)
    return hi, lo


def _router_kernel(x_ref, shift_ref, scale_ref, g_ref, w_ref, b_ref, h_ref, route_ref):
    h = _modulate(x_ref[0], g_ref[...], shift_ref[0], scale_ref[0])
    h_hi, h_lo = _split_bf16(h)
    h_ref[...] = h_hi
    w_hi, w_lo = _split_bf16(w_ref[...])
    logits = (jnp.dot(h_hi, w_hi, preferred_element_type=F32)
              + (jnp.dot(h_hi, w_lo, preferred_element_type=F32)
                 + jnp.dot(h_lo, w_hi, preferred_element_type=F32))) + b_ref[...]
    lane = lax.broadcasted_iota(jnp.int32, logits.shape, 1).astype(F32)
    big = float(LANES)

    def first_argmax(vals, valid):
        v = jnp.where(valid, vals, -jnp.inf)
        mx = jnp.max(v, axis=-1, keepdims=True)
        idx = jnp.min(jnp.where(valid & (v == mx), lane, big), axis=-1, keepdims=True)
        return mx, idx

    is_g = lane < N_GROUPS
    gmax, _ = first_argmax(logits, is_g)
    eg = jnp.where(is_g, jnp.exp(logits - gmax), 0.0)
    pg = eg / jnp.sum(eg, axis=-1, keepdims=True)
    pg_top, g_top = first_argmax(pg, is_g)
    e_lane0 = N_GROUPS + g_top * EXPERTS_PER_GROUP
    in_grp = (lane >= e_lane0) & (lane < e_lane0 + EXPERTS_PER_GROUP)
    w1, i1 = first_argmax(logits, in_grp)
    w2, i2 = first_argmax(logits, in_grp & (lane != i1))
    e2 = jnp.exp(w2 - w1)
    p1 = 1.0 / (1.0 + e2)
    p2 = e2 / (1.0 + e2)
    a = i1 - e_lane0
    b = i2 - e_lane0
    lo = jnp.minimum(a, b)
    hi = jnp.maximum(a, b)
    pair = lo * (7.0 - lo) * 0.5 + (hi - lo - 1.0)
    bucket = g_top * N_PAIRS + pair
    gate_lo = pg_top * jnp.where(a < b, p1, p2)
    gate_hi = pg_top * jnp.where(a < b, p2, p1)
    route_ref[...] = jnp.where(lane == 0.0, bucket, jnp.where(lane == 1.0, gate_lo, jnp.where(lane == 2.0, gate_hi, 0.0)))


def _router(x, shift, scale, g, w_r, b_r, h_all, route_all, row0):
    b, s, d = x.shape
    tm = _tile(s, 512)
    nt = s // tm
    assert row0 % tm == 0
    blk0 = row0 // tm
    r = shift.shape[1]
    rm = tm if r > 1 else 1
    mod_spec = pl.BlockSpec((1, rm, d), (lambda bb, i: (bb, i, 0)) if r > 1 else (lambda bb, i: (bb, 0, 0)))
    in_specs = [pl.BlockSpec((1, tm, d), lambda bb, i: (bb, i, 0)), mod_spec, mod_spec,
                pl.BlockSpec((1, d), lambda bb, i: (0, 0)),
                pl.BlockSpec((d, LANES), lambda bb, i: (0, 0)),
                pl.BlockSpec((1, LANES), lambda bb, i: (0, 0))]
    args = [x, shift, scale, g, w_r, b_r]
    aliases = {}
    kern = _router_kernel
    if h_all is not None:
        in_specs += [pl.BlockSpec(memory_space=pl.ANY), pl.BlockSpec(memory_space=pl.ANY)]
        args += [h_all, route_all]
        aliases = {6: 0, 7: 1}
        kern = lambda *refs: _router_kernel(*refs[:6], *refs[8:])
        t_rows = h_all.shape[0]
    else:
        t_rows = None
    return functools.partial(
        pl.pallas_call,
        kern,
        grid=(b, nt),
        in_specs=in_specs,
        out_specs=[pl.BlockSpec((tm, d), lambda bb, i: (blk0 + bb * nt + i, 0)),
                   pl.BlockSpec((tm, LANES), lambda bb, i: (blk0 + bb * nt + i, 0))],
        input_output_aliases=aliases,
        compiler_params=_params("arbitrary", "arbitrary"),
        name="moe_router",
    ), args, t_rows


def _expert_kernel(ea_ref, eb_ref, valid_ref, h_ref, gate_ref, w1a_ref, w3a_ref, w2a_ref,
                   w1b_ref, w3b_ref, w2b_ref, y_ref):
    t = pl.program_id(0)

    @pl.when(valid_ref[t] == 0)
    def _():
        y_ref[...] = jnp.zeros(y_ref.shape, F32)

    @pl.when(valid_ref[t] != 0)
    def _():
        h = h_ref[...]
        gate = gate_ref[...]

        def act(w1_ref, w3_ref, gcol):
            a = jnp.dot(h, w1_ref[0], preferred_element_type=F32)
            c = jnp.dot(h, w3_ref[0], preferred_element_type=F32)
            return (a * jax.nn.sigmoid(a) * c * gate[:, gcol:gcol + 1]).astype(BF16)

        y_ref[...] = (jnp.dot(act(w1a_ref, w3a_ref, 0), w2a_ref[0], preferred_element_type=F32)
                      + jnp.dot(act(w1b_ref, w3b_ref, 1), w2b_ref[0], preferred_element_type=F32))


def _experts(h_sorted, gate_sorted, tile_ea, tile_eb, tile_valid, w1_bf, w3_bf, w2_bf):
    p, d = h_sorted.shape
    tm = MXU_ROWS
    f = w1_bf.shape[-1]
    amap = lambda t, ea, eb, va: (ea[t], 0, 0)
    bmap = lambda t, ea, eb, va: (eb[t], 0, 0)
    grid_spec = pltpu.PrefetchScalarGridSpec(
        num_scalar_prefetch=3,
        grid=(p // tm,),
        in_specs=[pl.BlockSpec((tm, d), lambda t, ea, eb, va: (t, 0)),
                  pl.BlockSpec((tm, 2), lambda t, ea, eb, va: (t, 0)),
                  pl.BlockSpec((1, d, f), amap), pl.BlockSpec((1, d, f), amap), pl.BlockSpec((1, f, d), amap),
                  pl.BlockSpec((1, d, f), bmap), pl.BlockSpec((1, d, f), bmap), pl.BlockSpec((1, f, d), bmap)],
        out_specs=pl.BlockSpec((tm, d), lambda t, ea, eb, va: (t, 0)),
    )
    return pl.pallas_call(
        _expert_kernel,
        grid_spec=grid_spec,
        out_shape=jax.ShapeDtypeStruct((p, d), F32),
        compiler_params=_params("arbitrary"),
        name="moe_experts",
    )(tile_ea, tile_eb, tile_valid, h_sorted, gate_sorted, w1_bf, w3_bf, w2_bf, w1_bf, w3_bf, w2_bf)


def _schedule(bucket, n_tok, tm):
    n_tiles = (n_tok + N_BUCKETS * (tm - 1)) // tm + 1
    onehot = (bucket[:, None] == jnp.arange(N_BUCKETS, dtype=jnp.int32)[None, :]).astype(jnp.int32)
    csum = jnp.cumsum(onehot, axis=0)
    rank = jnp.sum(onehot * csum, axis=1) - 1
    counts = csum[-1]
    tiles_per = (counts + tm - 1) // tm
    tile_end = jnp.cumsum(tiles_per)
    tile_off = tile_end - tiles_per
    slot = (tile_off * tm)[bucket] + rank
    tok_of_slot = jnp.zeros((n_tiles * tm,), jnp.int32).at[slot].set(jnp.arange(n_tok, dtype=jnp.int32))
    tile_ids = jnp.arange(n_tiles, dtype=jnp.int32)
    tile_bucket = jnp.sum((tile_ids[:, None] >= tile_end[None, :]).astype(jnp.int32), axis=1)
    tile_valid = (tile_ids < tile_end[-1]).astype(jnp.int32)
    last = jnp.maximum(tile_end[-1] - 1, 0)
    tile_bucket = jnp.where(tile_valid == 1, tile_bucket, tile_bucket[last])
    grp = tile_bucket // N_PAIRS
    pair = tile_bucket % N_PAIRS
    tile_ea = grp * EXPERTS_PER_GROUP + jnp.asarray(_PAIR_LO, jnp.int32)[pair]
    tile_eb = grp * EXPERTS_PER_GROUP + jnp.asarray(_PAIR_HI, jnp.int32)[pair]
    return slot, tok_of_slot, tile_ea, tile_eb, tile_valid


def _combine_kernel(x_ref, y_ref, gate_ref, xo_ref):
    xo_ref[0] = x_ref[0] + gate_ref[0] * y_ref[...]


def _combine(x, y_tok, gate, row0):
    b, s, d = x.shape
    tm = _tile(s, 512)
    nt = s // tm
    blk0 = row0 // tm
    r = gate.shape[1]
    rm = tm if r > 1 else 1
    gate_spec = pl.BlockSpec((1, rm, d), (lambda bb, i: (bb, i, 0)) if r > 1 else (lambda bb, i: (bb, 0, 0)))
    return pl.pallas_call(
        _combine_kernel,
        grid=(b, nt),
        in_specs=[pl.BlockSpec((1, tm, d), lambda bb, i: (bb, i, 0)),
                  pl.BlockSpec((tm, d), lambda bb, i: (blk0 + bb * nt + i, 0)),
                  gate_spec],
        out_specs=pl.BlockSpec((1, tm, d), lambda bb, i: (bb, i, 0)),
        out_shape=jax.ShapeDtypeStruct((b, s, d), F32),
        compiler_params=_params("arbitrary", "arbitrary"),
        name="moe_combine",
    )(x, y_tok, gate)


def _moe(xp, xs_flat, mods_p, mods_s, g2, w_r, b_r, w1_bf, w3_bf, w2_bf):
    b, s, d = xp.shape
    ts = xs_flat.shape[1]
    n_p = b * s
    n_tok = n_p + ts
    call, args, _ = _router(xp, mods_p[0], mods_p[1], g2, w_r, b_r, None, None, 0)
    h_all, route_all = call(out_shape=[jax.ShapeDtypeStruct((n_tok, d), BF16),
                                       jax.ShapeDtypeStruct((n_tok, LANES), F32)])(*args)
    call, args, _ = _router(xs_flat, mods_s[0], mods_s[1], g2, w_r, b_r, h_all, route_all, n_p)
    h_all, route_all = call(out_shape=[jax.ShapeDtypeStruct((n_tok, d), BF16),
                                       jax.ShapeDtypeStruct((n_tok, LANES), F32)])(*args)
    bucket = route_all[:, 0].astype(jnp.int32)
    slot, tok_of_slot, tile_ea, tile_eb, tile_valid = _schedule(bucket, n_tok, MXU_ROWS)
    h_sorted = jnp.take(h_all, tok_of_slot, axis=0)
    gate_sorted = jnp.take(route_all[:, 1:3], tok_of_slot, axis=0)
    y_sorted = _experts(h_sorted, gate_sorted, tile_ea, tile_eb, tile_valid, w1_bf, w3_bf, w2_bf)
    y_tok = jnp.take(y_sorted, slot, axis=0)
    return _combine(xp, y_tok, mods_p[2], 0), _combine(xs_flat, y_tok, mods_s[2], n_p)


def _rope_tables(pos):
    half = HEAD_DIM // 2
    inv = ROPE_THETA ** (-jnp.arange(half, dtype=F32) / half)
    ang = pos.astype(F32)[:, None] * inv[None, :]
    cos, sin = jnp.cos(ang), jnp.sin(ang)
    return jnp.concatenate([cos, cos], axis=-1), jnp.concatenate([-sin, sin], axis=-1)


def kernel(x_prompt, x_sample, state_pool, cache_k, cache_v, page_table, c_prompt, c_sample, w_ada, b_ada, g_norm1, g_norm2, w_pool, b_pool, pool_scale, w_qkv, g_q, g_k, lam_q1, lam_k1, lam_q2, lam_k2, g_sub, w_o, w_rg, b_rg, w_re, b_re, w1, w3, w2):
    bp, seq, d = x_prompt.shape
    bd, dec_t, _ = x_sample.shape
    depth = w_ada.shape[0]
    n_heads = cache_k.shape[3]
    qk_width = n_heads * 2 * HEAD_DIM
    page = cache_k.shape[2]
    past_len = page_table.shape[1] * page
    ts = bd * dec_t
    rows = max(8, 1 << (dec_t - 1).bit_length())
    pool_state = state_pool.shape[2]

    n_c = bp + bd
    c_all = jnp.concatenate([c_prompt, c_sample, jnp.zeros((-(-n_c // 8) * 8 - n_c, d), F32)], axis=0)
    mod = _ada(c_all, w_ada, b_ada)

    cos_p, sin_p = _rope_tables(jnp.arange(seq))
    cos_s, sin_s = _rope_tables(past_len + jnp.arange(dec_t))
    cos_s, sin_s = jnp.tile(cos_s, (bd, 1)), jnp.tile(sin_s, (bd, 1))
    cache_k4 = cache_k.reshape(cache_k.shape[0], cache_k.shape[1], page, qk_width)
    cache_v4 = cache_v.reshape(cache_v.shape[0], cache_v.shape[1], page, n_heads * V_DIM)
    w_r = jnp.concatenate([w_rg, w_re, jnp.zeros((depth, d, LANES - w_rg.shape[-1] - w_re.shape[-1]), F32)], axis=-1)
    b_r = jnp.concatenate([b_rg, b_re, jnp.zeros((depth, LANES - b_rg.shape[-1] - b_re.shape[-1]), F32)], axis=-1)

    xp, xs = x_prompt, x_sample
    nk_p, nv_p, npool_p, nk_s, nv_s, npool_s = [], [], [], [], [], []
    for i in range(depth):
        chunks = [mod[i, :, c * d:(c + 1) * d] for c in range(6)]
        m_p = [c[:bp, None, :] for c in chunks]
        m_s = [c[bp:bp + bd, None, :] for c in chunks]
        m_s_rows = [jnp.repeat(c[bp:bp + bd], dec_t, axis=0)[None] for c in chunks]
        g1 = g_norm1[i][None]
        g2 = g_norm2[i][None]
        li = i // 2
        if i % 2 == 0:
            w_pool_bf = w_pool[li].astype(BF16)
            b_pool_i = b_pool[li].reshape(1, d)
            ls = pool_scale[li][None]
            xp, tail_p = _pool(xp, jnp.zeros((bp, HIST, d), F32), m_p[0], m_p[1], m_p[2], g1,
                               w_pool_bf, b_pool_i, ls, 0)
            npool_p.append(tail_p[:, HIST - pool_state:])
            hist_s = jnp.pad(state_pool[li], ((0, 0), (HIST - pool_state, 0), (0, 0)))
            xs_pad = jnp.pad(xs, ((0, 0), (0, rows - dec_t), (0, 0)))
            xs_new, tail_s = _pool(xs_pad, hist_s, m_s[0], m_s[1], m_s[2], g1, w_pool_bf, b_pool_i, ls, past_len)
            xs = xs_new[:, :dec_t]
            ext = jnp.concatenate([hist_s[:, :rows], tail_s], axis=1)
            npool_s.append(ext[:, HIST + dec_t - pool_state:HIST + dec_t])
        else:
            w_qkv_bf = w_qkv[li].astype(BF16)
            w_o_bf = w_o[li].astype(BF16)
            gq, gk, gs = g_q[li][None], g_k[li][None], g_sub[li][None]
            lam_vecs = jnp.stack([lam_q1[li], lam_k1[li], lam_q2[li], lam_k2[li]])
            q, k, k_bf, v, v_bf = _qkv(xp, m_p[0], m_p[1], g1, w_qkv_bf, gq, gk, cos_p, sin_p, qk_width)
            o = _prompt_attn(q, k_bf, v_bf, lam_vecs, gs, i)
            xp = _oproj(o, w_o_bf, xp, m_p[2])
            nk_p.append(k.reshape(bp, seq, n_heads, 2, HEAD_DIM))
            nv_p.append(v.reshape(bp, seq, n_heads, V_DIM))
            xs_flat = xs.reshape(1, ts, d)
            q, k, k_bf, v, v_bf = _qkv(xs_flat, m_s_rows[0], m_s_rows[1], g1, w_qkv_bf, gq, gk, cos_s, sin_s, qk_width)
            pad = lambda a, n: jnp.pad(a.reshape(bd, dec_t, -1), ((0, 0), (0, n - dec_t), (0, 0)))
            o = _decode_attn(pad(q.astype(F32), rows), pad(k_bf, LANES), pad(v_bf, LANES), cache_k4, cache_v4,
                             page_table, li, lam_vecs, gs, i, dec_t)
            o = o[:, :dec_t].reshape(1, ts, -1).astype(BF16)
            xs = _oproj(o, w_o_bf, xs_flat, m_s_rows[2]).reshape(bd, dec_t, d)
            nk_s.append(k.reshape(bd, dec_t, n_heads, 2, HEAD_DIM))
            nv_s.append(v.reshape(bd, dec_t, n_heads, V_DIM))
        xp, xs_flat = _moe(xp, xs.reshape(1, ts, d), m_p[3:], m_s_rows[3:], g2, w_r[i], b_r[i][None],
                           w1[i].astype(BF16), w3[i].astype(BF16), w2[i].astype(BF16))
        xs = xs_flat.reshape(bd, dec_t, d)
    return (xp, xs, jnp.stack(nk_p), jnp.stack(nv_p), jnp.stack(npool_p),
            jnp.stack(nk_s), jnp.stack(nv_s), jnp.stack(npool_s))
```

```python
import functools
import math

import jax
import jax.numpy as jnp
import numpy as np
from jax import lax
from jax.experimental import pallas as pl
from jax.experimental.pallas import tpu as pltpu

F32 = jnp.float32
BF16 = jnp.bfloat16

EPS = 1e-6
NEG = -1e30
ROPE_THETA = 10000.0
POOL_WINDOWS = (2, 4, 8, 16)
HIST = 16
HEAD_DIM = 128
V_DIM = 2 * HEAD_DIM
N_GROUPS = 4
EXPERTS_PER_GROUP = 4
N_PAIRS = 6
N_BUCKETS = N_GROUPS * N_PAIRS
LANES = 128
MXU_ROWS = 256
VMEM_LIMIT = 48 * 1024 * 1024

_PAIR_LO = (0, 0, 0, 1, 1, 2)
_PAIR_HI = (1, 2, 3, 2, 3, 3)


def _lambda_init(layer):
    return 0.8 - 0.6 * math.exp(-0.3 * layer)


def _tile(n, pref):
    if n <= pref:
        return n
    t = pref
    while t >= 8:
        if n % t == 0 and t % 8 == 0:
            return t
        t -= 8
    return n


def _params(*sem):
    return pltpu.CompilerParams(dimension_semantics=sem, vmem_limit_bytes=VMEM_LIMIT)


def _modulate(x, g, shift, scale):
    ms = jnp.mean(x * x, axis=-1, keepdims=True)
    y = x * lax.rsqrt(ms + EPS) * g
    return y * (1.0 + scale) + shift


def _ada_kernel(c_ref, w_ref, b_ref, o_ref):
    c = c_ref[...]
    s = (c * jax.nn.sigmoid(c)).astype(BF16)
    o_ref[0] = jnp.dot(s, w_ref[0].astype(BF16), preferred_element_type=F32) + b_ref[0]


def _ada(c_all, w_ada, b_ada):
    depth, d, n = w_ada.shape
    r = c_all.shape[0]
    tn = _tile(n, 1024)
    return pl.pallas_call(
        _ada_kernel,
        grid=(depth, n // tn),
        in_specs=[pl.BlockSpec((r, d), lambda l, j: (0, 0)),
                  pl.BlockSpec((1, d, tn), lambda l, j: (l, 0, j)),
                  pl.BlockSpec((1, 1, tn), lambda l, j: (l, 0, j))],
        out_specs=pl.BlockSpec((1, r, tn), lambda l, j: (l, 0, j)),
        out_shape=jax.ShapeDtypeStruct((depth, r, n), F32),
        compiler_params=_params("arbitrary", "arbitrary"),
        name="ada",
    )(c_all, w_ada, b_ada.reshape(depth, 1, n))


def _pool_kernel(x_ref, hist_ref, shift_ref, scale_ref, gate_ref, g_ref, w_ref, b_ref, ls_ref,
                 xo_ref, tail_ref, hbuf, *, tm, pos0, n_tiles):
    j = pl.program_id(1)
    d = x_ref.shape[-1]
    grp = d // len(POOL_WINDOWS)

    @pl.when(j == 0)
    def _():
        hbuf[0:HIST, :] = hist_ref[0]

    x = x_ref[0]
    h = _modulate(x, g_ref[...], shift_ref[0], scale_ref[0])
    hbuf[HIST:HIST + tm, :] = h
    pos = pos0 + j * tm + lax.broadcasted_iota(jnp.int32, (tm, 1), 0)
    gate = gate_ref[0]
    for gi, w in enumerate(POOL_WINDOWS):
        lo = gi * grp
        s = hbuf[:, lo:lo + grp]
        k = 1
        while k < w:
            s = s + pltpu.roll(s, k, 0)
            k *= 2
        cnt = jnp.minimum(pos + 1, w).astype(F32)
        dlt = s[HIST:, :] / cnt - h[:, lo:lo + grp]
        y = jnp.dot(dlt.astype(BF16), w_ref[gi], preferred_element_type=F32) + b_ref[:, lo:lo + grp]
        y = y * ls_ref[:, lo:lo + grp]
        xo_ref[0, :, lo:lo + grp] = x[:, lo:lo + grp] + gate[:, lo:lo + grp] * y

    @pl.when(j == n_tiles - 1)
    def _():
        tail_ref[0] = hbuf[tm:tm + HIST, :]

    if n_tiles > 1:
        @pl.when(j < n_tiles - 1)
        def _():
            hbuf[0:HIST, :] = hbuf[tm:tm + HIST, :]


def _pool(x, hist, shift, scale, gate, g, w_pool_bf, b_pool, ls, pos0):
    b, s, d = x.shape
    tm = _tile(s, 512)
    n_tiles = s // tm
    assert n_tiles == 1 or tm >= HIST
    ng, grp, _ = w_pool_bf.shape
    row = lambda bb, j: (bb, 0, 0)
    return pl.pallas_call(
        functools.partial(_pool_kernel, tm=tm, pos0=pos0, n_tiles=n_tiles),
        grid=(b, n_tiles),
        in_specs=[pl.BlockSpec((1, tm, d), lambda bb, j: (bb, j, 0)),
                  pl.BlockSpec((1, HIST, d), row),
                  pl.BlockSpec((1, 1, d), row), pl.BlockSpec((1, 1, d), row), pl.BlockSpec((1, 1, d), row),
                  pl.BlockSpec((1, d), lambda bb, j: (0, 0)),
                  pl.BlockSpec((ng, grp, grp), lambda bb, j: (0, 0, 0)),
                  pl.BlockSpec((1, d), lambda bb, j: (0, 0)),
                  pl.BlockSpec((1, d), lambda bb, j: (0, 0))],
        out_specs=[pl.BlockSpec((1, tm, d), lambda bb, j: (bb, j, 0)),
                   pl.BlockSpec((1, HIST, d), row)],
        out_shape=[jax.ShapeDtypeStruct((b, s, d), F32), jax.ShapeDtypeStruct((b, HIST, d), F32)],
        scratch_shapes=[pltpu.VMEM((HIST + tm, d), F32)],
        compiler_params=_params("arbitrary", "arbitrary"),
        name="pool_mixer",
    )(x, hist, shift, scale, gate, g, w_pool_bf, b_pool, ls)


def _qkv_kernel(x_ref, shift_ref, scale_ref, g_ref, w_ref, gq_ref, gk_ref, cos_ref, sin_ref, *rest,
                nq, nk, nv, q_scale, n_alias):
    q_ref, kb_ref, vb_ref, k_ref, v_ref, h_scr = rest[n_alias:]
    n = pl.program_id(2)
    tm = x_ref.shape[1]
    tn = w_ref.shape[2]
    cpb = tn // HEAD_DIM
    hpb = tn // V_DIM
    n_chunks = nk * cpb
    n_heads = nv * hpb

    @pl.when(n == 0)
    def _():
        h_scr[...] = _modulate(x_ref[0], g_ref[...], shift_ref[0], scale_ref[0]).astype(BF16)

    acc = jnp.dot(h_scr[...], w_ref[0], preferred_element_type=F32)

    def norm_rope(c, g):
        xc = acc[:, c * HEAD_DIM:(c + 1) * HEAD_DIM]
        y = xc * lax.rsqrt(jnp.mean(xc * xc, axis=-1, keepdims=True) + EPS) * g
        return y * cos_ref[...] + pltpu.roll(y, HEAD_DIM // 2, 1) * sin_ref[...]

    @pl.when(n < nq)
    def _():
        for c in range(cpb):
            q_ref[0, :, c * HEAD_DIM:(c + 1) * HEAD_DIM] = (norm_rope(c, gq_ref[...]) * q_scale).astype(BF16)

    for kn in range(nk):
        @pl.when(n == nq + kn)
        def _(kn=kn):
            for c in range(cpb):
                y = norm_rope(c, gk_ref[...])
                kb_ref[0, :, c * HEAD_DIM:(c + 1) * HEAD_DIM] = y.astype(BF16)
                k_ref[0, 0, pl.ds(kn * cpb + c, tm, stride=n_chunks), :] = y

    for vn in range(nv):
        @pl.when(n == nq + nk + vn)
        def _(vn=vn):
            vb_ref[0] = acc.astype(BF16)
            for c in range(hpb):
                for half in range(V_DIM // LANES):
                    lo = c * V_DIM + half * LANES
                    v_ref[0, 0, pl.ds(half * n_heads + vn * hpb + c, tm, stride=2 * n_heads), :] = acc[:, lo:lo + LANES]


def _qkv(x, shift, scale, g, w_qkv_bf, g_q, g_k, cos, sin, qk_width, n_layers, lidx, k_nat, v_nat):
    b, s, d = x.shape
    n_all = w_qkv_bf.shape[2]
    v_width = n_all - 2 * qk_width
    tm = _tile(s, 512)
    tn = 512 if (qk_width % 512 == 0 and v_width % 512 == 0) else V_DIM
    nq = nk = qk_width // tn
    nv = v_width // tn
    n_chunks = qk_width // HEAD_DIM
    n_heads = v_width // V_DIM
    r = shift.shape[1]
    rm = tm if r > 1 else 1
    mod_spec = pl.BlockSpec((1, rm, d), (lambda bb, i, n: (bb, i, 0)) if r > 1 else (lambda bb, i, n: (bb, 0, 0)))
    qmap = lambda bb, i, n: (bb, i, jnp.minimum(n, nq - 1))
    kmap = lambda bb, i, n: (bb, i, jnp.clip(n - nq, 0, nk - 1))
    vmap = lambda bb, i, n: (bb, i, jnp.clip(n - nq - nk, 0, nv - 1))
    nat_map = lambda bb, i, n: (lidx, bb, i, 0)
    tab_spec = pl.BlockSpec((tm, HEAD_DIM), lambda bb, i, n: (i, 0))
    vec_spec = pl.BlockSpec((1, HEAD_DIM), lambda bb, i, n: (0, 0))
    in_specs = [pl.BlockSpec((1, tm, d), lambda bb, i, n: (bb, i, 0)),
                mod_spec, mod_spec,
                pl.BlockSpec((1, d), lambda bb, i, n: (0, 0)),
                pl.BlockSpec((1, d, tn), lambda bb, i, n: (lidx, 0, n)),
                vec_spec, vec_spec, tab_spec, tab_spec]
    args = [x, shift, scale, g, w_qkv_bf, g_q, g_k, cos, sin]
    aliases = {}
    if k_nat is not None:
        in_specs += [pl.BlockSpec(memory_space=pl.ANY), pl.BlockSpec(memory_space=pl.ANY)]
        args += [k_nat, v_nat]
        aliases = {9: 3, 10: 4}
    return pl.pallas_call(
        functools.partial(_qkv_kernel, nq=nq, nk=nk, nv=nv, q_scale=HEAD_DIM ** -0.5 * math.log2(math.e),
                          n_alias=len(aliases)),
        grid=(b, s // tm, nq + nk + nv),
        in_specs=in_specs,
        out_specs=[pl.BlockSpec((1, tm, tn), qmap), pl.BlockSpec((1, tm, tn), kmap), pl.BlockSpec((1, tm, tn), vmap),
                   pl.BlockSpec((1, 1, tm * n_chunks, HEAD_DIM), nat_map),
                   pl.BlockSpec((1, 1, tm * 2 * n_heads, LANES), nat_map)],
        out_shape=[jax.ShapeDtypeStruct((b, s, qk_width), BF16), jax.ShapeDtypeStruct((b, s, qk_width), BF16),
                   jax.ShapeDtypeStruct((b, s, v_width), BF16),
                   jax.ShapeDtypeStruct((n_layers, b, s * n_chunks, HEAD_DIM), F32),
                   jax.ShapeDtypeStruct((n_layers, b, s * 2 * n_heads, LANES), F32)],
        scratch_shapes=[pltpu.VMEM((tm, d), BF16)],
        input_output_aliases=aliases,
        compiler_params=_params("arbitrary", "arbitrary", "arbitrary"),
        name="qkv_proj",
    )(*args)


def _lam_value(lam_ref, lam_init):
    l = lam_ref[...]
    a = jnp.sum(l[0:1] * l[1:2], axis=-1, keepdims=True)
    b = jnp.sum(l[2:3] * l[3:4], axis=-1, keepdims=True)
    return jnp.exp(a) - jnp.exp(b) + lam_init


def _softmax_step(s, v_bf, m_prev, l_prev, acc_prev):
    m_new = jnp.maximum(m_prev, jnp.max(s, axis=-1, keepdims=True))
    alpha = jnp.exp2(m_prev - m_new)
    p = jnp.exp2(s - m_new)
    l_new = alpha * l_prev + jnp.sum(p, axis=-1, keepdims=True)
    acc_new = alpha * acc_prev + jnp.dot(p.astype(BF16), v_bf, preferred_element_type=F32)
    return m_new, l_new, acc_new


def _head_out(o0, o1, lam, gsub, out_scale):
    o = o0 - lam * o1
    return o * lax.rsqrt(jnp.mean(o * o, axis=-1, keepdims=True) + EPS) * gsub * out_scale


def _attn_kernel(qi_ref, kj_ref, q_ref, k_ref, v_ref, lam_ref, gsub_ref, o_ref, m_scr, l_scr, acc_scr,
                 *, lam_init, out_scale):
    t = pl.program_id(2)
    i = qi_ref[t]
    j = kj_ref[t]

    @pl.when(j == 0)
    def _():
        m_scr[...] = jnp.full(m_scr.shape, NEG, F32)
        l_scr[...] = jnp.zeros(l_scr.shape, F32)
        acc_scr[...] = jnp.zeros(acc_scr.shape, F32)

    nt = (((1,), (1,)), ((), ()))

    def step(causal):
        for mp in range(2):
            cols = slice(mp * HEAD_DIM, (mp + 1) * HEAD_DIM)
            s = lax.dot_general(q_ref[0, :, cols], k_ref[0, :, cols], nt, preferred_element_type=F32)
            if causal:
                row = lax.broadcasted_iota(jnp.int32, s.shape, 0)
                col = lax.broadcasted_iota(jnp.int32, s.shape, 1)
                s = jnp.where(col <= row, s, NEG)
            m, l, acc = _softmax_step(s, v_ref[0], m_scr[mp], l_scr[mp], acc_scr[mp])
            m_scr[mp] = m
            l_scr[mp] = l
            acc_scr[mp] = acc

    @pl.when(j < i)
    def _():
        step(False)

    @pl.when(j == i)
    def _():
        step(True)
        lam = _lam_value(lam_ref, lam_init)
        o = _head_out(acc_scr[0] / l_scr[0], acc_scr[1] / l_scr[1], lam, gsub_ref[...], out_scale)
        o_ref[0] = o.astype(BF16)


def _prompt_attn(q_bf, k_bf, v_bf, lam_vecs, g_sub, layer):
    b, s, qk = q_bf.shape
    h = qk // (2 * HEAD_DIM)
    tq = _tile(s, 512)
    nb = s // tq
    qi = np.array([i for i in range(nb) for _ in range(i + 1)], np.int32)
    kj = np.array([j for i in range(nb) for j in range(i + 1)], np.int32)
    grid_spec = pltpu.PrefetchScalarGridSpec(
        num_scalar_prefetch=2,
        grid=(b, h, len(qi)),
        in_specs=[pl.BlockSpec((1, tq, 2 * HEAD_DIM), lambda bb, hh, t, qi_r, kj_r: (bb, qi_r[t], hh)),
                  pl.BlockSpec((1, tq, 2 * HEAD_DIM), lambda bb, hh, t, qi_r, kj_r: (bb, kj_r[t], hh)),
                  pl.BlockSpec((1, tq, V_DIM), lambda bb, hh, t, qi_r, kj_r: (bb, kj_r[t], hh)),
                  pl.BlockSpec((4, HEAD_DIM), lambda bb, hh, t, qi_r, kj_r: (0, 0)),
                  pl.BlockSpec((1, V_DIM), lambda bb, hh, t, qi_r, kj_r: (0, 0))],
        out_specs=pl.BlockSpec((1, tq, V_DIM), lambda bb, hh, t, qi_r, kj_r: (bb, qi_r[t], hh)),
        scratch_shapes=[pltpu.VMEM((2, tq, 1), F32), pltpu.VMEM((2, tq, 1), F32),
                        pltpu.VMEM((2, tq, V_DIM), F32)],
    )
    return pl.pallas_call(
        functools.partial(_attn_kernel, lam_init=_lambda_init(layer), out_scale=1.0 - _lambda_init(layer)),
        grid_spec=grid_spec,
        out_shape=jax.ShapeDtypeStruct((b, s, h * V_DIM), BF16),
        compiler_params=_params("arbitrary", "arbitrary", "arbitrary"),
        name="prompt_attn",
    )(jnp.asarray(qi), jnp.asarray(kj), q_bf, k_bf, v_bf, lam_vecs, g_sub)


def _decode_kernel(pt_ref, q_ref, kn_ref, vn_ref, kc_ref, vc_ref, lam_ref, gsub_ref, o_ref,
                   bias_scr, m_scr, l_scr, acc_scr, *, n_heads, rows, t_new, lam_init, out_scale):
    j = pl.program_id(1)
    n_pages = pl.num_programs(1)
    hq = n_heads * rows
    log_rows = rows.bit_length() - 1
    log_heads = n_heads.bit_length() - 1
    nt = (((1,), (1,)), ((), ()))

    def row_head(shape):
        r = lax.broadcasted_iota(jnp.int32, shape, 0)
        return jnp.right_shift(jnp.bitwise_and(r, hq - 1), log_rows)

    @pl.when(j == 0)
    def _():
        c = lax.broadcasted_iota(jnp.int32, bias_scr.shape, 1)
        bias_scr[...] = jnp.where(row_head(bias_scr.shape) == jnp.bitwise_and(c, n_heads - 1), 0.0, NEG)
        m_scr[...] = jnp.full(m_scr.shape, NEG, F32)
        l_scr[...] = jnp.zeros(l_scr.shape, F32)
        acc_scr[...] = jnp.zeros(acc_scr.shape, F32)

    def scores(k0, k1):
        q = q_ref[0]
        return jnp.concatenate([lax.dot_general(q[:hq], k0, nt, preferred_element_type=F32),
                                lax.dot_general(q[hq:], k1, nt, preferred_element_type=F32)], axis=0)

    def step(s, v_bf):
        m, l, acc = _softmax_step(s, v_bf, m_scr[...], l_scr[...], acc_scr[...])
        m_scr[...] = m
        l_scr[...] = l
        acc_scr[...] = acc

    half = kc_ref.shape[2] // 2
    k0 = kc_ref[0, 0, pl.ds(0, half, stride=2), :].astype(BF16)
    k1 = kc_ref[0, 0, pl.ds(1, half, stride=2), :].astype(BF16)
    step(scores(k0, k1) + bias_scr[...], vc_ref[0, 0].astype(BF16))

    @pl.when(j == n_pages - 1)
    def _():
        s_new = scores(kn_ref[0, 0], kn_ref[0, 1])
        r = lax.broadcasted_iota(jnp.int32, s_new.shape, 0)
        c = lax.broadcasted_iota(jnp.int32, s_new.shape, 1)
        key = jnp.right_shift(c, log_heads)
        ok = ((row_head(s_new.shape) == jnp.bitwise_and(c, n_heads - 1))
              & (key <= jnp.bitwise_and(r, rows - 1)) & (key < t_new))
        step(jnp.where(ok, s_new, NEG), vn_ref[0])
        lam = _lam_value(lam_ref, lam_init)
        acc = acc_scr[...]
        l = l_scr[...]
        o_ref[0] = _head_out(acc[:hq] / l[:hq], acc[hq:] / l[hq:], lam, gsub_ref[...], out_scale)


def _decode_attn(q_bf, k_bf, v_bf, cache_k, cache_v, page_table, li, lam_vecs, g_sub, layer, rows):
    bd, t_new, h = q_bf.shape[:3]
    assert rows & (rows - 1) == 0 and h & (h - 1) == 0 and t_new <= rows
    keys = max(rows, LANES // h)
    n_pages = page_table.shape[1]
    pad_t = lambda a, n: jnp.pad(a, ((0, 0), (0, n - t_new)) + ((0, 0),) * (a.ndim - 2))
    q2 = pad_t(q_bf, rows).transpose(0, 3, 2, 1, 4).reshape(bd, 2 * h * rows, HEAD_DIM)
    kn = pad_t(k_bf, keys).transpose(0, 3, 1, 2, 4).reshape(bd, 2, keys * h, HEAD_DIM)
    vn = pad_t(v_bf, keys).reshape(bd, keys * h, V_DIM)
    cmap = lambda bb, j, pt: (li, pt[bb * n_pages + j], 0, 0)
    grid_spec = pltpu.PrefetchScalarGridSpec(
        num_scalar_prefetch=1,
        grid=(bd, n_pages),
        in_specs=[pl.BlockSpec((1, 2 * h * rows, HEAD_DIM), lambda bb, j, pt: (bb, 0, 0)),
                  pl.BlockSpec((1, 2, keys * h, HEAD_DIM), lambda bb, j, pt: (bb, 0, 0, 0)),
                  pl.BlockSpec((1, keys * h, V_DIM), lambda bb, j, pt: (bb, 0, 0)),
                  pl.BlockSpec((1, 1) + cache_k.shape[2:], cmap), pl.BlockSpec((1, 1) + cache_v.shape[2:], cmap),
                  pl.BlockSpec((4, HEAD_DIM), lambda bb, j, pt: (0, 0)),
                  pl.BlockSpec((1, V_DIM), lambda bb, j, pt: (0, 0))],
        out_specs=pl.BlockSpec((1, h * rows, V_DIM), lambda bb, j, pt: (bb, 0, 0)),
        scratch_shapes=[pltpu.VMEM((2 * h * rows, cache_v.shape[2]), F32),
                        pltpu.VMEM((2 * h * rows, 1), F32), pltpu.VMEM((2 * h * rows, 1), F32),
                        pltpu.VMEM((2 * h * rows, V_DIM), F32)],
    )
    o = pl.pallas_call(
        functools.partial(_decode_kernel, n_heads=h, rows=rows, t_new=t_new, lam_init=_lambda_init(layer),
                          out_scale=1.0 - _lambda_init(layer)),
        grid_spec=grid_spec,
        out_shape=jax.ShapeDtypeStruct((bd, h * rows, V_DIM), F32),
        compiler_params=_params("arbitrary", "arbitrary"),
        name="decode_attn",
    )(page_table.reshape(-1), q2, kn, vn, cache_k, cache_v, lam_vecs, g_sub)
    o = o.reshape(bd, h, rows, V_DIM).transpose(0, 2, 1, 3)[:, :t_new]
    return o.reshape(bd, t_new, h * V_DIM).astype(BF16)


def _oproj_kernel(o_ref, w_ref, x_ref, gate_ref, xo_ref):
    y = jnp.dot(o_ref[0], w_ref[0], preferred_element_type=F32)
    xo_ref[0] = x_ref[0] + gate_ref[0] * y


def _oproj(o_bf, w_o_bf, li, x, gate):
    b, s, d = x.shape
    kdim = o_bf.shape[-1]
    tm = _tile(s, 512)
    r = gate.shape[1]
    rm = tm if r > 1 else 1
    gate_spec = pl.BlockSpec((1, rm, d), (lambda bb, i: (bb, i, 0)) if r > 1 else (lambda bb, i: (bb, 0, 0)))
    return pl.pallas_call(
        _oproj_kernel,
        grid=(b, s // tm),
        in_specs=[pl.BlockSpec((1, tm, kdim), lambda bb, i: (bb, i, 0)),
                  pl.BlockSpec((1, kdim, d), lambda bb, i: (li, 0, 0)),
                  pl.BlockSpec((1, tm, d), lambda bb, i: (bb, i, 0)),
                  gate_spec],
        out_specs=pl.BlockSpec((1, tm, d), lambda bb, i: (bb, i, 0)),
        out_shape=jax.ShapeDtypeStruct((b, s, d), F32),
        compiler_params=_params("arbitrary", "arbitrary"),
        name="out_proj",
    )(o_bf, w_o_bf, x, gate)


def _split_bf16(a):
    hi = a.astype(BF16)
    lo = (a - hi.astype(F32)).astype(BF16)
    return hi, lo


def _router_kernel(x_ref, shift_ref, scale_ref, g_ref, w_ref, b_ref, *rest):
    h_ref = rest[-1]
    d = x_ref.shape[-1]
    h = _modulate(x_ref[0], g_ref[...], shift_ref[0], scale_ref[0])
    h_hi, h_lo = _split_bf16(h)
    h_ref[:, :d] = h
    w_hi, w_lo = _split_bf16(w_ref[...])
    logits = (jnp.dot(h_hi, w_hi, preferred_element_type=F32)
              + (jnp.dot(h_hi, w_lo, preferred_element_type=F32)
                 + jnp.dot(h_lo, w_hi, preferred_element_type=F32))) + b_ref[...]
    lane = lax.broadcasted_iota(jnp.int32, logits.shape, 1).astype(F32)
    big = float(LANES)

    def first_argmax(vals, valid):
        v = jnp.where(valid, vals, -jnp.inf)
        mx = jnp.max(v, axis=-1, keepdims=True)
        idx = jnp.min(jnp.where(valid & (v == mx), lane, big), axis=-1, keepdims=True)
        return mx, idx

    is_g = lane < N_GROUPS
    gmax, _ = first_argmax(logits, is_g)
    eg = jnp.where(is_g, jnp.exp(logits - gmax), 0.0)
    pg = eg / jnp.sum(eg, axis=-1, keepdims=True)
    pg_top, g_top = first_argmax(pg, is_g)
    e_lane0 = N_GROUPS + g_top * EXPERTS_PER_GROUP
    in_grp = (lane >= e_lane0) & (lane < e_lane0 + EXPERTS_PER_GROUP)
    w1, i1 = first_argmax(logits, in_grp)
    w2, i2 = first_argmax(logits, in_grp & (lane != i1))
    e2 = jnp.exp(w2 - w1)
    p1 = 1.0 / (1.0 + e2)
    p2 = e2 / (1.0 + e2)
    a = i1 - e_lane0
    b = i2 - e_lane0
    lo = jnp.minimum(a, b)
    hi = jnp.maximum(a, b)
    pair = lo * (7.0 - lo) * 0.5 + (hi - lo - 1.0)
    bucket = g_top * N_PAIRS + pair
    gate_lo = pg_top * jnp.where(a < b, p1, p2)
    gate_hi = pg_top * jnp.where(a < b, p2, p1)
    h_ref[:, d:] = jnp.where(lane == 0.0, bucket, jnp.where(lane == 1.0, gate_lo, jnp.where(lane == 2.0, gate_hi, 0.0)))


def _router(x, shift, scale, g, w_r, b_r, h_all, n_tok, row0):
    b, s, d = x.shape
    tm = _tile(s, 512)
    nt = s // tm
    assert row0 % tm == 0
    blk0 = row0 // tm
    r = shift.shape[1]
    rm = tm if r > 1 else 1
    mod_spec = pl.BlockSpec((1, rm, d), (lambda bb, i: (bb, i, 0)) if r > 1 else (lambda bb, i: (bb, 0, 0)))
    in_specs = [pl.BlockSpec((1, tm, d), lambda bb, i: (bb, i, 0)), mod_spec, mod_spec,
                pl.BlockSpec((1, d), lambda bb, i: (0, 0)),
                pl.BlockSpec((d, LANES), lambda bb, i: (0, 0)),
                pl.BlockSpec((1, LANES), lambda bb, i: (0, 0))]
    args = [x, shift, scale, g, w_r, b_r]
    aliases = {}
    if h_all is not None:
        in_specs += [pl.BlockSpec(memory_space=pl.ANY)]
        args += [h_all]
        aliases = {6: 0}
    return pl.pallas_call(
        _router_kernel,
        grid=(b, nt),
        in_specs=in_specs,
        out_specs=pl.BlockSpec((tm, d + LANES), lambda bb, i: (blk0 + bb * nt + i, 0)),
        out_shape=jax.ShapeDtypeStruct((n_tok, d + LANES), F32),
        input_output_aliases=aliases,
        compiler_params=_params("arbitrary", "arbitrary"),
        name="moe_router",
    )(*args)


def _expert_kernel(ea_ref, eb_ref, nval_ref, tok_ref, h_hbm, w1a_ref, w3a_ref, w2a_ref,
                   w1b_ref, w3b_ref, w2b_ref, y_hbm, hbuf, ybuf, gsem, ssem, *, tm, n_tok):
    t = pl.program_id(0)
    n_tiles = pl.num_programs(0)
    slot = lax.rem(t, 2)
    d = y_hbm.shape[1]

    def gather_row(tile, sl, r):
        tok = jnp.minimum(tok_ref[tile * tm + r], n_tok - 1)
        return pltpu.make_async_copy(h_hbm.at[pl.ds(tok, 1), :], hbuf.at[sl, pl.ds(r, 1), :], gsem.at[sl])

    def scatter_row(tile, sl, r):
        tok = tok_ref[tile * tm + r]
        return pltpu.make_async_copy(ybuf.at[sl, pl.ds(r, 1), :], y_hbm.at[pl.ds(tok, 1), :], ssem.at[sl])

    def gather_start(tile, sl):
        def body(r, c):
            gather_row(tile, sl, r).start()
            return c
        lax.fori_loop(0, tm, body, 0)

    def gather_wait(tile, sl):
        def body(r, c):
            gather_row(tile, sl, r).wait()
            return c
        lax.fori_loop(0, tm, body, 0)

    def scatter_start(tile, sl):
        def body(r, c):
            scatter_row(tile, sl, r).start()
            return c
        lax.fori_loop(0, nval_ref[tile], body, 0)

    def scatter_wait(tile, sl):
        def body(r, c):
            scatter_row(tile, sl, r).wait()
            return c
        lax.fori_loop(0, nval_ref[tile], body, 0)

    @pl.when(t == 0)
    def _():
        gather_start(0, 0)

    @pl.when(t + 1 < n_tiles)
    def _():
        gather_start(t + 1, 1 - slot)

    gather_wait(t, slot)

    @pl.when(t >= 2)
    def _():
        scatter_wait(t - 2, slot)

    @pl.when(nval_ref[t] > 0)
    def _():
        rows = hbuf[slot]
        h = rows[:, :d].astype(BF16)
        gate = rows[:, d:]

        def act(w1_ref, w3_ref, gcol):
            a = jnp.dot(h, w1_ref[0, 0], preferred_element_type=F32)
            c = jnp.dot(h, w3_ref[0, 0], preferred_element_type=F32)
            return (a * jax.nn.sigmoid(a) * c * gate[:, gcol:gcol + 1]).astype(BF16)

        ybuf[slot] = (jnp.dot(act(w1a_ref, w3a_ref, 1), w2a_ref[0, 0], preferred_element_type=F32)
                      + jnp.dot(act(w1b_ref, w3b_ref, 2), w2b_ref[0, 0], preferred_element_type=F32))

    scatter_start(t, slot)

    @pl.when(t == n_tiles - 1)
    def _():
        scatter_wait(t, slot)

        @pl.when(t >= 1)
        def _():
            scatter_wait(t - 1, 1 - slot)


def _experts(h_all, tok_of_slot, tile_ea, tile_eb, tile_nval, w1_bf, w3_bf, w2_bf, layer, n_tok):
    width = h_all.shape[1]
    d = width - LANES
    tm = MXU_ROWS
    n_tiles = tile_ea.shape[0]
    f = w1_bf.shape[-1]
    amap = lambda t, ea, eb, nv, tok: (layer, ea[t], 0, 0)
    bmap = lambda t, ea, eb, nv, tok: (layer, eb[t], 0, 0)
    grid_spec = pltpu.PrefetchScalarGridSpec(
        num_scalar_prefetch=4,
        grid=(n_tiles,),
        in_specs=[pl.BlockSpec(memory_space=pl.ANY),
                  pl.BlockSpec((1, 1, d, f), amap), pl.BlockSpec((1, 1, d, f), amap), pl.BlockSpec((1, 1, f, d), amap),
                  pl.BlockSpec((1, 1, d, f), bmap), pl.BlockSpec((1, 1, d, f), bmap), pl.BlockSpec((1, 1, f, d), bmap)],
        out_specs=pl.BlockSpec(memory_space=pl.ANY),
        scratch_shapes=[pltpu.VMEM((2, tm, width), F32), pltpu.VMEM((2, tm, d), F32),
                        pltpu.SemaphoreType.DMA((2,)), pltpu.SemaphoreType.DMA((2,))],
    )
    return pl.pallas_call(
        functools.partial(_expert_kernel, tm=tm, n_tok=n_tok),
        grid_spec=grid_spec,
        out_shape=jax.ShapeDtypeStruct((n_tok, d), F32),
        compiler_params=_params("arbitrary"),
        name="moe_experts",
    )(tile_ea, tile_eb, tile_nval, tok_of_slot, h_all, w1_bf, w3_bf, w2_bf, w1_bf, w3_bf, w2_bf)


def _schedule(bucket, n_tok, tm):
    n_tiles = (n_tok + N_BUCKETS * (tm - 1)) // tm + 1
    onehot = (bucket[:, None] == jnp.arange(N_BUCKETS, dtype=jnp.int32)[None, :]).astype(jnp.int32)
    csum = jnp.cumsum(onehot, axis=0)
    rank = jnp.sum(onehot * csum, axis=1) - 1
    counts = csum[-1]
    tiles_per = (counts + tm - 1) // tm
    tile_end = jnp.cumsum(tiles_per)
    tile_off = tile_end - tiles_per
    slot = (tile_off * tm)[bucket] + rank
    tok_of_slot = jnp.full((n_tiles * tm,), n_tok, jnp.int32).at[slot].set(jnp.arange(n_tok, dtype=jnp.int32))
    tile_ids = jnp.arange(n_tiles, dtype=jnp.int32)
    tile_bucket = jnp.sum((tile_ids[:, None] >= tile_end[None, :]).astype(jnp.int32), axis=1)
    used = tile_ids < tile_end[-1]
    last = jnp.maximum(tile_end[-1] - 1, 0)
    tile_bucket = jnp.where(used, tile_bucket, tile_bucket[last])
    tile_nval = jnp.where(used, jnp.clip(counts[tile_bucket] - (tile_ids - tile_off[tile_bucket]) * tm, 0, tm), 0)
    grp = tile_bucket // N_PAIRS
    pair = tile_bucket % N_PAIRS
    tile_ea = grp * EXPERTS_PER_GROUP + jnp.asarray(_PAIR_LO, jnp.int32)[pair]
    tile_eb = grp * EXPERTS_PER_GROUP + jnp.asarray(_PAIR_HI, jnp.int32)[pair]
    return tok_of_slot, tile_ea, tile_eb, tile_nval.astype(jnp.int32)


def _combine_kernel(x_ref, y_ref, gate_ref, xo_ref):
    xo_ref[0] = x_ref[0] + gate_ref[0] * y_ref[...]


def _combine(x, y_tok, gate, row0):
    b, s, d = x.shape
    tm = _tile(s, 512)
    nt = s // tm
    blk0 = row0 // tm
    r = gate.shape[1]
    rm = tm if r > 1 else 1
    gate_spec = pl.BlockSpec((1, rm, d), (lambda bb, i: (bb, i, 0)) if r > 1 else (lambda bb, i: (bb, 0, 0)))
    return pl.pallas_call(
        _combine_kernel,
        grid=(b, nt),
        in_specs=[pl.BlockSpec((1, tm, d), lambda bb, i: (bb, i, 0)),
                  pl.BlockSpec((tm, d), lambda bb, i: (blk0 + bb * nt + i, 0)),
                  gate_spec],
        out_specs=pl.BlockSpec((1, tm, d), lambda bb, i: (bb, i, 0)),
        out_shape=jax.ShapeDtypeStruct((b, s, d), F32),
        compiler_params=_params("arbitrary", "arbitrary"),
        name="moe_combine",
    )(x, y_tok, gate)


def _moe(xp, xs_flat, mods_p, mods_s, g2, w_r, b_r, w1_bf, w3_bf, w2_bf, layer):
    b, s, d = xp.shape
    ts = xs_flat.shape[1]
    n_p = b * s
    n_tok = n_p + ts
    h_all = _router(xp, mods_p[0], mods_p[1], g2, w_r, b_r, None, n_tok, 0)
    h_all = _router(xs_flat, mods_s[0], mods_s[1], g2, w_r, b_r, h_all, n_tok, n_p)
    bucket = h_all[:, d].astype(jnp.int32)
    tok_of_slot, tile_ea, tile_eb, tile_nval = _schedule(bucket, n_tok, MXU_ROWS)
    y_tok = _experts(h_all, tok_of_slot, tile_ea, tile_eb, tile_nval, w1_bf, w3_bf, w2_bf, layer, n_tok)
    return _combine(xp, y_tok, mods_p[2], 0), _combine(xs_flat, y_tok, mods_s[2], n_p)


def _rope_tables(pos):
    half = HEAD_DIM // 2
    inv = ROPE_THETA ** (-jnp.arange(half, dtype=F32) / half)
    ang = pos.astype(F32)[:, None] * inv[None, :]
    cos, sin = jnp.cos(ang), jnp.sin(ang)
    return jnp.concatenate([cos, cos], axis=-1), jnp.concatenate([-sin, sin], axis=-1)


def kernel(x_prompt, x_sample, state_pool, cache_k, cache_v, page_table, c_prompt, c_sample, w_ada, b_ada, g_norm1, g_norm2, w_pool, b_pool, pool_scale, w_qkv, g_q, g_k, lam_q1, lam_k1, lam_q2, lam_k2, g_sub, w_o, w_rg, b_rg, w_re, b_re, w1, w3, w2):
    bp, seq, d = x_prompt.shape
    bd, dec_t, _ = x_sample.shape
    depth = w_ada.shape[0]
    n_heads = cache_k.shape[3]
    qk_width = n_heads * 2 * HEAD_DIM
    page = cache_k.shape[2]
    past_len = page_table.shape[1] * page
    ts = bd * dec_t
    rows = max(8, 1 << (dec_t - 1).bit_length())
    pool_state = state_pool.shape[2]

    n_c = bp + bd
    c_all = jnp.concatenate([c_prompt, c_sample, jnp.zeros((-(-n_c // 8) * 8 - n_c, d), F32)], axis=0)
    mod = _ada(c_all, w_ada, b_ada)

    cos_p, sin_p = _rope_tables(jnp.arange(seq))
    cos_s, sin_s = _rope_tables(past_len + jnp.arange(dec_t))
    cos_s, sin_s = jnp.tile(cos_s, (bd, 1)), jnp.tile(sin_s, (bd, 1))
    cache_k2 = cache_k.reshape(cache_k.shape[0], cache_k.shape[1], page * n_heads * 2, HEAD_DIM)
    cache_v2 = cache_v.reshape(cache_v.shape[0], cache_v.shape[1], page * n_heads, V_DIM)
    n_attn = depth // 2
    k_nat_p = v_nat_p = k_nat_s = v_nat_s = None
    w1_bf, w3_bf, w2_bf = w1.astype(BF16), w3.astype(BF16), w2.astype(BF16)
    w_qkv_bf, w_o_bf = w_qkv.astype(BF16), w_o.astype(BF16)
    w_r = jnp.concatenate([w_rg, w_re, jnp.zeros((depth, d, LANES - w_rg.shape[-1] - w_re.shape[-1]), F32)], axis=-1)
    b_r = jnp.concatenate([b_rg, b_re, jnp.zeros((depth, LANES - b_rg.shape[-1] - b_re.shape[-1]), F32)], axis=-1)

    xp, xs = x_prompt, x_sample
    npool_p, npool_s = [], []
    for i in range(depth):
        chunks = [mod[i, :, c * d:(c + 1) * d] for c in range(6)]
        m_p = [c[:bp, None, :] for c in chunks]
        m_s = [c[bp:bp + bd, None, :] for c in chunks]
        m_s_rows = [jnp.repeat(c[bp:bp + bd], dec_t, axis=0)[None] for c in chunks]
        g1 = g_norm1[i][None]
        g2 = g_norm2[i][None]
        li = i // 2
        if i % 2 == 0:
            w_pool_bf = w_pool[li].astype(BF16)
            b_pool_i = b_pool[li].reshape(1, d)
            ls = pool_scale[li][None]
            xp, tail_p = _pool(xp, jnp.zeros((bp, HIST, d), F32), m_p[0], m_p[1], m_p[2], g1,
                               w_pool_bf, b_pool_i, ls, 0)
            npool_p.append(tail_p[:, HIST - pool_state:])
            hist_s = jnp.pad(state_pool[li], ((0, 0), (HIST - pool_state, 0), (0, 0)))
            xs_pad = jnp.pad(xs, ((0, 0), (0, rows - dec_t), (0, 0)))
            xs_new, tail_s = _pool(xs_pad, hist_s, m_s[0], m_s[1], m_s[2], g1, w_pool_bf, b_pool_i, ls, past_len)
            xs = xs_new[:, :dec_t]
            ext = jnp.concatenate([hist_s[:, :rows], tail_s], axis=1)
            npool_s.append(ext[:, HIST + dec_t - pool_state:HIST + dec_t])
        else:
            gq, gk, gs = g_q[li][None], g_k[li][None], g_sub[li][None]
            lam_vecs = jnp.stack([lam_q1[li], lam_k1[li], lam_q2[li], lam_k2[li]])
            q, k_bf, v_bf, k_nat_p, v_nat_p = _qkv(xp, m_p[0], m_p[1], g1, w_qkv_bf, gq, gk, cos_p, sin_p, qk_width,
                                                   n_attn, li, k_nat_p, v_nat_p)
            o = _prompt_attn(q, k_bf, v_bf, lam_vecs, gs, i)
            xp = _oproj(o, w_o_bf, li, xp, m_p[2])
            xs_flat = xs.reshape(1, ts, d)
            q, k_bf, v_bf, k_nat_s, v_nat_s = _qkv(xs_flat, m_s_rows[0], m_s_rows[1], g1, w_qkv_bf, gq, gk,
                                                   cos_s, sin_s, qk_width, n_attn, li, k_nat_s, v_nat_s)
            o = _decode_attn(q.reshape(bd, dec_t, n_heads, 2, HEAD_DIM), k_bf.reshape(bd, dec_t, n_heads, 2, HEAD_DIM),
                             v_bf.reshape(bd, dec_t, n_heads, V_DIM), cache_k2, cache_v2,
                             page_table, li, lam_vecs, gs, i, rows)
            xs = _oproj(o.reshape(1, ts, -1), w_o_bf, li, xs_flat, m_s_rows[2]).reshape(bd, dec_t, d)
        xp, xs_flat = _moe(xp, xs.reshape(1, ts, d), m_p[3:], m_s_rows[3:], g2, w_r[i], b_r[i][None],
                           w1_bf, w3_bf, w2_bf, i)
        xs = xs_flat.reshape(bd, dec_t, d)
    def v_out(v_nat, b, s):
        v = v_nat.reshape(n_attn, b, s, V_DIM // LANES, n_heads, LANES)
        return v.transpose(0, 1, 2, 4, 3, 5).reshape(n_attn, b, s, n_heads, V_DIM)

    return (xp, xs,
            k_nat_p.reshape(n_attn, bp, seq, n_heads, 2, HEAD_DIM), v_out(v_nat_p, bp, seq),
            jnp.stack(npool_p),
            k_nat_s.reshape(n_attn, bd, dec_t, n_heads, 2, HEAD_DIM), v_out(v_nat_s, bd, dec_t),
            jnp.stack(npool_s))
```

```python
import functools
import math

import jax
import jax.numpy as jnp
import numpy as np
from jax import lax
from jax.experimental import pallas as pl
from jax.experimental.pallas import tpu as pltpu

F32 = jnp.float32
BF16 = jnp.bfloat16

EPS = 1e-6
NEG = -1e30
ROPE_THETA = 10000.0
POOL_WINDOWS = (2, 4, 8, 16)
HIST = 16
HEAD_DIM = 128
V_DIM = 2 * HEAD_DIM
N_GROUPS = 4
EXPERTS_PER_GROUP = 4
N_PAIRS = 6
N_BUCKETS = N_GROUPS * N_PAIRS
LANES = 128
MXU_ROWS = 256
ROW_DMA_UNROLL = 8
DECODE_PAGES_PER_STEP = 4
VMEM_LIMIT = 48 * 1024 * 1024

_PAIR_LO = (0, 0, 0, 1, 1, 2)
_PAIR_HI = (1, 2, 3, 2, 3, 3)


def _lambda_init(layer):
    return 0.8 - 0.6 * math.exp(-0.3 * layer)


def _tile(n, pref):
    if n <= pref:
        return n
    t = pref
    while t >= 8:
        if n % t == 0 and t % 8 == 0:
            return t
        t -= 8
    return n


def _params(*sem):
    return pltpu.CompilerParams(dimension_semantics=sem, vmem_limit_bytes=VMEM_LIMIT)


def _modulate(x, g, shift, scale):
    ms = jnp.mean(x * x, axis=-1, keepdims=True)
    y = x * lax.rsqrt(ms + EPS) * g
    return y * (1.0 + scale) + shift


def _ada_kernel(c_ref, w_ref, b_ref, o_ref):
    c = c_ref[...]
    s = (c * jax.nn.sigmoid(c)).astype(BF16)
    o_ref[0] = jnp.dot(s, w_ref[0].astype(BF16), preferred_element_type=F32) + b_ref[0]


def _ada(c_all, w_ada, b_ada):
    depth, d, n = w_ada.shape
    r = c_all.shape[0]
    tn = _tile(n, 1024)
    return pl.pallas_call(
        _ada_kernel,
        grid=(depth, n // tn),
        in_specs=[pl.BlockSpec((r, d), lambda l, j: (0, 0)),
                  pl.BlockSpec((1, d, tn), lambda l, j: (l, 0, j)),
                  pl.BlockSpec((1, 1, tn), lambda l, j: (l, 0, j))],
        out_specs=pl.BlockSpec((1, r, tn), lambda l, j: (l, 0, j)),
        out_shape=jax.ShapeDtypeStruct((depth, r, n), F32),
        compiler_params=_params("arbitrary", "arbitrary"),
        name="ada",
    )(c_all, w_ada, b_ada.reshape(depth, 1, n))


def _pool_kernel(x_ref, hist_ref, shift_ref, scale_ref, gate_ref, g_ref, w_ref, b_ref, ls_ref,
                 xo_ref, tail_ref, hbuf, *, tm, pos0, n_tiles):
    j = pl.program_id(1)
    d = x_ref.shape[-1]
    grp = d // len(POOL_WINDOWS)

    @pl.when(j == 0)
    def _():
        hbuf[0:HIST, :] = hist_ref[0]

    x = x_ref[0]
    h = _modulate(x, g_ref[...], shift_ref[0], scale_ref[0])
    hbuf[HIST:HIST + tm, :] = h
    pos = pos0 + j * tm + lax.broadcasted_iota(jnp.int32, (tm, 1), 0)
    gate = gate_ref[0]
    for gi, w in enumerate(POOL_WINDOWS):
        lo = gi * grp
        s = hbuf[:, lo:lo + grp]
        k = 1
        while k < w:
            s = s + pltpu.roll(s, k, 0)
            k *= 2
        cnt = jnp.minimum(pos + 1, w).astype(F32)
        dlt = s[HIST:, :] / cnt - h[:, lo:lo + grp]
        y = jnp.dot(dlt.astype(BF16), w_ref[gi], preferred_element_type=F32) + b_ref[:, lo:lo + grp]
        y = y * ls_ref[:, lo:lo + grp]
        xo_ref[0, :, lo:lo + grp] = x[:, lo:lo + grp] + gate[:, lo:lo + grp] * y

    @pl.when(j == n_tiles - 1)
    def _():
        tail_ref[0] = hbuf[tm:tm + HIST, :]

    if n_tiles > 1:
        @pl.when(j < n_tiles - 1)
        def _():
            hbuf[0:HIST, :] = hbuf[tm:tm + HIST, :]


def _pool(x, hist, shift, scale, gate, g, w_pool_bf, b_pool, ls, pos0):
    b, s, d = x.shape
    tm = _tile(s, 512)
    n_tiles = s // tm
    assert n_tiles == 1 or tm >= HIST
    ng, grp, _ = w_pool_bf.shape
    row = lambda bb, j: (bb, 0, 0)
    return pl.pallas_call(
        functools.partial(_pool_kernel, tm=tm, pos0=pos0, n_tiles=n_tiles),
        grid=(b, n_tiles),
        in_specs=[pl.BlockSpec((1, tm, d), lambda bb, j: (bb, j, 0)),
                  pl.BlockSpec((1, HIST, d), row),
                  pl.BlockSpec((1, 1, d), row), pl.BlockSpec((1, 1, d), row), pl.BlockSpec((1, 1, d), row),
                  pl.BlockSpec((1, d), lambda bb, j: (0, 0)),
                  pl.BlockSpec((ng, grp, grp), lambda bb, j: (0, 0, 0)),
                  pl.BlockSpec((1, d), lambda bb, j: (0, 0)),
                  pl.BlockSpec((1, d), lambda bb, j: (0, 0))],
        out_specs=[pl.BlockSpec((1, tm, d), lambda bb, j: (bb, j, 0)),
                   pl.BlockSpec((1, HIST, d), row)],
        out_shape=[jax.ShapeDtypeStruct((b, s, d), F32), jax.ShapeDtypeStruct((b, HIST, d), F32)],
        scratch_shapes=[pltpu.VMEM((HIST + tm, d), F32)],
        compiler_params=_params("arbitrary", "arbitrary"),
        name="pool_mixer",
    )(x, hist, shift, scale, gate, g, w_pool_bf, b_pool, ls)


def _qkv_kernel(x_ref, shift_ref, scale_ref, g_ref, w_ref, gq_ref, gk_ref, cos_ref, sin_ref, *rest,
                nq, nk, nv, q_scale, n_alias, v_t):
    q_ref, kb_ref, vb_ref, k_ref, v_ref, h_scr = rest[n_alias:]
    n = pl.program_id(2)
    tm = x_ref.shape[1]
    tn = w_ref.shape[2]
    cpb = tn // HEAD_DIM
    hpb = tn // V_DIM
    n_chunks = nk * cpb
    n_heads = nv * hpb

    @pl.when(n == 0)
    def _():
        h_scr[...] = _modulate(x_ref[0], g_ref[...], shift_ref[0], scale_ref[0]).astype(BF16)

    acc = jnp.dot(h_scr[...], w_ref[0], preferred_element_type=F32)

    def norm_rope(c, g):
        xc = acc[:, c * HEAD_DIM:(c + 1) * HEAD_DIM]
        y = xc * lax.rsqrt(jnp.mean(xc * xc, axis=-1, keepdims=True) + EPS) * g
        return y * cos_ref[...] + pltpu.roll(y, HEAD_DIM // 2, 1) * sin_ref[...]

    @pl.when(n < nq)
    def _():
        for c in range(cpb):
            q_ref[0, :, c * HEAD_DIM:(c + 1) * HEAD_DIM] = (norm_rope(c, gq_ref[...]) * q_scale).astype(BF16)

    for kn in range(nk):
        @pl.when(n == nq + kn)
        def _(kn=kn):
            for c in range(cpb):
                y = norm_rope(c, gk_ref[...])
                kb_ref[0, :, c * HEAD_DIM:(c + 1) * HEAD_DIM] = y.astype(BF16)
                k_ref[0, 0, pl.ds(kn * cpb + c, tm, stride=n_chunks), :] = y

    for vn in range(nv):
        @pl.when(n == nq + nk + vn)
        def _(vn=vn):
            if v_t:
                for c in range(hpb):
                    vb_ref[0, c] = acc[:, c * V_DIM:(c + 1) * V_DIM].T.astype(BF16)
            else:
                vb_ref[0] = acc.astype(BF16)
            for c in range(hpb):
                for half in range(V_DIM // LANES):
                    lo = c * V_DIM + half * LANES
                    v_ref[0, 0, pl.ds(half * n_heads + vn * hpb + c, tm, stride=2 * n_heads), :] = acc[:, lo:lo + LANES]


def _qkv(x, shift, scale, g, w_qkv_bf, g_q, g_k, cos, sin, qk_width, n_layers, lidx, k_nat, v_nat, v_t):
    b, s, d = x.shape
    n_all = w_qkv_bf.shape[2]
    v_width = n_all - 2 * qk_width
    tm = _tile(s, 512)
    tn = 512 if (qk_width % 512 == 0 and v_width % 512 == 0) else V_DIM
    nq = nk = qk_width // tn
    nv = v_width // tn
    n_chunks = qk_width // HEAD_DIM
    n_heads = v_width // V_DIM
    r = shift.shape[1]
    rm = tm if r > 1 else 1
    mod_spec = pl.BlockSpec((1, rm, d), (lambda bb, i, n: (bb, i, 0)) if r > 1 else (lambda bb, i, n: (bb, 0, 0)))
    qmap = lambda bb, i, n: (bb, i, jnp.minimum(n, nq - 1))
    kmap = lambda bb, i, n: (bb, i, jnp.clip(n - nq, 0, nk - 1))
    vmap = lambda bb, i, n: (bb, i, jnp.clip(n - nq - nk, 0, nv - 1))
    nat_map = lambda bb, i, n: (lidx, bb, i, 0)
    tab_spec = pl.BlockSpec((tm, HEAD_DIM), lambda bb, i, n: (i, 0))
    vec_spec = pl.BlockSpec((1, HEAD_DIM), lambda bb, i, n: (0, 0))
    in_specs = [pl.BlockSpec((1, tm, d), lambda bb, i, n: (bb, i, 0)),
                mod_spec, mod_spec,
                pl.BlockSpec((1, d), lambda bb, i, n: (0, 0)),
                pl.BlockSpec((1, d, tn), lambda bb, i, n: (lidx, 0, n)),
                vec_spec, vec_spec, tab_spec, tab_spec]
    args = [x, shift, scale, g, w_qkv_bf, g_q, g_k, cos, sin]
    aliases = {}
    if k_nat is not None:
        in_specs += [pl.BlockSpec(memory_space=pl.ANY), pl.BlockSpec(memory_space=pl.ANY)]
        args += [k_nat, v_nat]
        aliases = {9: 3, 10: 4}
    if v_t:
        hpb = tn // V_DIM
        vb_spec = pl.BlockSpec((1, hpb, V_DIM, tm), lambda bb, i, n: (bb, jnp.clip(n - nq - nk, 0, nv - 1), 0, i))
        vb_shape = jax.ShapeDtypeStruct((b, n_heads, V_DIM, s), BF16)
    else:
        vb_spec = pl.BlockSpec((1, tm, tn), vmap)
        vb_shape = jax.ShapeDtypeStruct((b, s, v_width), BF16)
    return pl.pallas_call(
        functools.partial(_qkv_kernel, nq=nq, nk=nk, nv=nv, q_scale=HEAD_DIM ** -0.5 * math.log2(math.e),
                          n_alias=len(aliases), v_t=v_t),
        grid=(b, s // tm, nq + nk + nv),
        in_specs=in_specs,
        out_specs=[pl.BlockSpec((1, tm, tn), qmap), pl.BlockSpec((1, tm, tn), kmap), vb_spec,
                   pl.BlockSpec((1, 1, tm * n_chunks, HEAD_DIM), nat_map),
                   pl.BlockSpec((1, 1, tm * 2 * n_heads, LANES), nat_map)],
        out_shape=[jax.ShapeDtypeStruct((b, s, qk_width), BF16), jax.ShapeDtypeStruct((b, s, qk_width), BF16),
                   vb_shape,
                   jax.ShapeDtypeStruct((n_layers, b, s * n_chunks, HEAD_DIM), F32),
                   jax.ShapeDtypeStruct((n_layers, b, s * 2 * n_heads, LANES), F32)],
        scratch_shapes=[pltpu.VMEM((tm, d), BF16)],
        input_output_aliases=aliases,
        compiler_params=_params("arbitrary", "arbitrary", "arbitrary"),
        name="qkv_proj",
    )(*args)


def _lam_value(lam_ref, lam_init):
    l = lam_ref[...]
    a = jnp.sum(l[0:1] * l[1:2], axis=-1, keepdims=True)
    b = jnp.sum(l[2:3] * l[3:4], axis=-1, keepdims=True)
    return jnp.exp(a) - jnp.exp(b) + lam_init


def _softmax_step(s_blocks, v_blocks, m_prev, l_prev, acc_prev):
    m_new = m_prev
    for s in s_blocks:
        m_new = jnp.maximum(m_new, jnp.max(s, axis=-1, keepdims=True))
    alpha = jnp.exp2(m_prev - m_new)
    l_new = alpha * l_prev
    acc_new = alpha * acc_prev
    for s, v_bf in zip(s_blocks, v_blocks):
        p = jnp.exp2(s - m_new)
        l_new = l_new + jnp.sum(p, axis=-1, keepdims=True)
        acc_new = acc_new + jnp.dot(p.astype(BF16), v_bf, preferred_element_type=F32)
    return m_new, l_new, acc_new


def _head_out(o0, o1, lam, gsub, out_scale):
    o = o0 - lam * o1
    return o * lax.rsqrt(jnp.mean(o * o, axis=-1, keepdims=True) + EPS) * gsub * out_scale


def _attn_kernel(qi_ref, kj_ref, q_ref, k_ref, vt_ref, lam_ref, gsub_ref, o_ref, m_scr, l_scr, acc_scr,
                 *, lam_init, out_scale):
    t = pl.program_id(2)
    i = qi_ref[t]
    j = kj_ref[t]

    @pl.when(j == 0)
    def _():
        m_scr[...] = jnp.full(m_scr.shape, NEG, F32)
        l_scr[...] = jnp.zeros(l_scr.shape, F32)
        acc_scr[...] = jnp.zeros(acc_scr.shape, F32)

    nt = (((1,), (1,)), ((), ()))

    def step(causal):
        vt = vt_ref[0, 0]
        for mp in range(2):
            cols = slice(mp * HEAD_DIM, (mp + 1) * HEAD_DIM)
            s = lax.dot_general(k_ref[0, :, cols], q_ref[0, :, cols], nt, preferred_element_type=F32)
            if causal:
                key = lax.broadcasted_iota(jnp.int32, s.shape, 0)
                qry = lax.broadcasted_iota(jnp.int32, s.shape, 1)
                s = jnp.where(key <= qry, s, NEG)
            m_prev = m_scr[mp]
            m_new = jnp.maximum(m_prev, jnp.max(s, axis=0, keepdims=True))
            alpha = jnp.exp2(m_prev - m_new)
            p = jnp.exp2(s - m_new)
            l_scr[mp] = alpha * l_scr[mp] + jnp.sum(p, axis=0, keepdims=True)
            acc_scr[mp] = alpha * acc_scr[mp] + jnp.dot(vt, p.astype(BF16), preferred_element_type=F32)
            m_scr[mp] = m_new

    @pl.when(j < i)
    def _():
        step(False)

    @pl.when(j == i)
    def _():
        step(True)
        lam = _lam_value(lam_ref, lam_init)
        o = acc_scr[0] / l_scr[0] - lam * (acc_scr[1] / l_scr[1])
        o = o * lax.rsqrt(jnp.mean(o * o, axis=0, keepdims=True) + EPS) * gsub_ref[...] * out_scale
        o_ref[0] = o.T.astype(BF16)


def _prompt_attn(q_bf, k_bf, vt_bf, lam_vecs, g_sub_col, layer):
    b, s, qk = q_bf.shape
    h = qk // (2 * HEAD_DIM)
    tq = _tile(s, 512)
    nb = s // tq
    qi = np.array([i for i in range(nb) for _ in range(i + 1)], np.int32)
    kj = np.array([j for i in range(nb) for j in range(i + 1)], np.int32)
    grid_spec = pltpu.PrefetchScalarGridSpec(
        num_scalar_prefetch=2,
        grid=(b, h, len(qi)),
        in_specs=[pl.BlockSpec((1, tq, 2 * HEAD_DIM), lambda bb, hh, t, qi_r, kj_r: (bb, qi_r[t], hh)),
                  pl.BlockSpec((1, tq, 2 * HEAD_DIM), lambda bb, hh, t, qi_r, kj_r: (bb, kj_r[t], hh)),
                  pl.BlockSpec((1, 1, V_DIM, tq), lambda bb, hh, t, qi_r, kj_r: (bb, hh, 0, kj_r[t])),
                  pl.BlockSpec((4, HEAD_DIM), lambda bb, hh, t, qi_r, kj_r: (0, 0)),
                  pl.BlockSpec((V_DIM, 1), lambda bb, hh, t, qi_r, kj_r: (0, 0))],
        out_specs=pl.BlockSpec((1, tq, V_DIM), lambda bb, hh, t, qi_r, kj_r: (bb, qi_r[t], hh)),
        scratch_shapes=[pltpu.VMEM((2, 1, tq), F32), pltpu.VMEM((2, 1, tq), F32),
                        pltpu.VMEM((2, V_DIM, tq), F32)],
    )
    return pl.pallas_call(
        functools.partial(_attn_kernel, lam_init=_lambda_init(layer), out_scale=1.0 - _lambda_init(layer)),
        grid_spec=grid_spec,
        out_shape=jax.ShapeDtypeStruct((b, s, h * V_DIM), BF16),
        compiler_params=_params("arbitrary", "arbitrary", "arbitrary"),
        name="prompt_attn",
    )(jnp.asarray(qi), jnp.asarray(kj), q_bf, k_bf, vt_bf, lam_vecs, g_sub_col)


def _decode_kernel(pt_ref, q_ref, kn_ref, vn_ref, *rest, ppb, n_heads, rows, t_new, lam_init, out_scale):
    kc_refs, vc_refs = rest[:ppb], rest[ppb:2 * ppb]
    lam_ref, gsub_ref, o_ref, bias_scr, m_scr, l_scr, acc_scr = rest[2 * ppb:]
    j = pl.program_id(1)
    n_steps = pl.num_programs(1)
    hq = n_heads * rows
    log_rows = rows.bit_length() - 1
    log_heads = n_heads.bit_length() - 1
    nt = (((1,), (1,)), ((), ()))

    def row_head(shape):
        r = lax.broadcasted_iota(jnp.int32, shape, 0)
        return jnp.right_shift(jnp.bitwise_and(r, hq - 1), log_rows)

    @pl.when(j == 0)
    def _():
        c = lax.broadcasted_iota(jnp.int32, bias_scr.shape, 1)
        bias_scr[...] = jnp.where(row_head(bias_scr.shape) == jnp.bitwise_and(c, n_heads - 1), 0.0, NEG)
        m_scr[...] = jnp.full(m_scr.shape, NEG, F32)
        l_scr[...] = jnp.zeros(l_scr.shape, F32)
        acc_scr[...] = jnp.zeros(acc_scr.shape, F32)

    def scores(k0, k1):
        q = q_ref[0]
        return jnp.concatenate([lax.dot_general(q[:hq], k0, nt, preferred_element_type=F32),
                                lax.dot_general(q[hq:], k1, nt, preferred_element_type=F32)], axis=0)

    def step(s_blocks, v_blocks):
        m, l, acc = _softmax_step(s_blocks, v_blocks, m_scr[...], l_scr[...], acc_scr[...])
        m_scr[...] = m
        l_scr[...] = l
        acc_scr[...] = acc

    half = kc_refs[0].shape[2] // 2
    s_blocks = []
    for kc_ref in kc_refs:
        k0 = kc_ref[0, 0, pl.ds(0, half, stride=2), :].astype(BF16)
        k1 = kc_ref[0, 0, pl.ds(1, half, stride=2), :].astype(BF16)
        s_blocks.append(scores(k0, k1) + bias_scr[...])
    step(s_blocks, [vc_ref[0, 0].astype(BF16) for vc_ref in vc_refs])

    @pl.when(j == n_steps - 1)
    def _():
        s_new = scores(kn_ref[0, 0], kn_ref[0, 1])
        r = lax.broadcasted_iota(jnp.int32, s_new.shape, 0)
        c = lax.broadcasted_iota(jnp.int32, s_new.shape, 1)
        key = jnp.right_shift(c, log_heads)
        ok = ((row_head(s_new.shape) == jnp.bitwise_and(c, n_heads - 1))
              & (key <= jnp.bitwise_and(r, rows - 1)) & (key < t_new))
        step([jnp.where(ok, s_new, NEG)], [vn_ref[0]])
        lam = _lam_value(lam_ref, lam_init)
        acc = acc_scr[...]
        l = l_scr[...]
        o_ref[0] = _head_out(acc[:hq] / l[:hq], acc[hq:] / l[hq:], lam, gsub_ref[...], out_scale)


def _decode_attn(q_bf, k_bf, v_bf, cache_k, cache_v, page_table, li, lam_vecs, g_sub, layer, rows):
    bd, t_new, h = q_bf.shape[:3]
    assert rows & (rows - 1) == 0 and h & (h - 1) == 0 and t_new <= rows
    keys = max(rows, LANES // h)
    n_pages = page_table.shape[1]
    pad_t = lambda a, n: jnp.pad(a, ((0, 0), (0, n - t_new)) + ((0, 0),) * (a.ndim - 2))
    q2 = pad_t(q_bf, rows).transpose(0, 3, 2, 1, 4).reshape(bd, 2 * h * rows, HEAD_DIM)
    kn = pad_t(k_bf, keys).transpose(0, 3, 1, 2, 4).reshape(bd, 2, keys * h, HEAD_DIM)
    vn = pad_t(v_bf, keys).reshape(bd, keys * h, V_DIM)
    ppb = max(p for p in range(1, DECODE_PAGES_PER_STEP + 1) if n_pages % p == 0)
    cmaps = [(lambda bb, j, pt, u=u: (li, pt[bb * n_pages + j * ppb + u], 0, 0)) for u in range(ppb)]
    grid_spec = pltpu.PrefetchScalarGridSpec(
        num_scalar_prefetch=1,
        grid=(bd, n_pages // ppb),
        in_specs=[pl.BlockSpec((1, 2 * h * rows, HEAD_DIM), lambda bb, j, pt: (bb, 0, 0)),
                  pl.BlockSpec((1, 2, keys * h, HEAD_DIM), lambda bb, j, pt: (bb, 0, 0, 0)),
                  pl.BlockSpec((1, keys * h, V_DIM), lambda bb, j, pt: (bb, 0, 0))]
                 + [pl.BlockSpec((1, 1) + cache_k.shape[2:], cm) for cm in cmaps]
                 + [pl.BlockSpec((1, 1) + cache_v.shape[2:], cm) for cm in cmaps]
                 + [pl.BlockSpec((4, HEAD_DIM), lambda bb, j, pt: (0, 0)),
                    pl.BlockSpec((1, V_DIM), lambda bb, j, pt: (0, 0))],
        out_specs=pl.BlockSpec((1, h * rows, V_DIM), lambda bb, j, pt: (bb, 0, 0)),
        scratch_shapes=[pltpu.VMEM((2 * h * rows, cache_v.shape[2]), F32),
                        pltpu.VMEM((2 * h * rows, 1), F32), pltpu.VMEM((2 * h * rows, 1), F32),
                        pltpu.VMEM((2 * h * rows, V_DIM), F32)],
    )
    o = pl.pallas_call(
        functools.partial(_decode_kernel, ppb=ppb, n_heads=h, rows=rows, t_new=t_new, lam_init=_lambda_init(layer),
                          out_scale=1.0 - _lambda_init(layer)),
        grid_spec=grid_spec,
        out_shape=jax.ShapeDtypeStruct((bd, h * rows, V_DIM), F32),
        compiler_params=_params("arbitrary", "arbitrary"),
        name="decode_attn",
    )(page_table.reshape(-1), q2, kn, vn, *([cache_k] * ppb), *([cache_v] * ppb), lam_vecs, g_sub)
    o = o.reshape(bd, h, rows, V_DIM).transpose(0, 2, 1, 3)[:, :t_new]
    return o.reshape(bd, t_new, h * V_DIM).astype(BF16)


def _oproj_kernel(o_ref, w_ref, x_ref, gate_ref, xo_ref):
    y = jnp.dot(o_ref[0], w_ref[0], preferred_element_type=F32)
    xo_ref[0] = x_ref[0] + gate_ref[0] * y


def _oproj(o_bf, w_o_bf, li, x, gate):
    b, s, d = x.shape
    kdim = o_bf.shape[-1]
    tm = _tile(s, 512)
    r = gate.shape[1]
    rm = tm if r > 1 else 1
    gate_spec = pl.BlockSpec((1, rm, d), (lambda bb, i: (bb, i, 0)) if r > 1 else (lambda bb, i: (bb, 0, 0)))
    return pl.pallas_call(
        _oproj_kernel,
        grid=(b, s // tm),
        in_specs=[pl.BlockSpec((1, tm, kdim), lambda bb, i: (bb, i, 0)),
                  pl.BlockSpec((1, kdim, d), lambda bb, i: (li, 0, 0)),
                  pl.BlockSpec((1, tm, d), lambda bb, i: (bb, i, 0)),
                  gate_spec],
        out_specs=pl.BlockSpec((1, tm, d), lambda bb, i: (bb, i, 0)),
        out_shape=jax.ShapeDtypeStruct((b, s, d), F32),
        compiler_params=_params("arbitrary", "arbitrary"),
        name="out_proj",
    )(o_bf, w_o_bf, x, gate)


def _split_bf16(a):
    hi = a.astype(BF16)
    lo = (a - hi.astype(F32)).astype(BF16)
    return hi, lo


def _router_kernel(x_ref, shift_ref, scale_ref, g_ref, w_ref, b_ref, *rest):
    h_ref = rest[-1]
    d = x_ref.shape[-1]
    h = _modulate(x_ref[0], g_ref[...], shift_ref[0], scale_ref[0])
    h_hi, h_lo = _split_bf16(h)
    h_ref[:, :d] = h
    w_hi, w_lo = _split_bf16(w_ref[...])
    logits = (jnp.dot(h_hi, w_hi, preferred_element_type=F32)
              + (jnp.dot(h_hi, w_lo, preferred_element_type=F32)
                 + jnp.dot(h_lo, w_hi, preferred_element_type=F32))) + b_ref[...]
    lane = lax.broadcasted_iota(jnp.int32, logits.shape, 1).astype(F32)
    big = float(LANES)

    def first_argmax(vals, valid):
        v = jnp.where(valid, vals, -jnp.inf)
        mx = jnp.max(v, axis=-1, keepdims=True)
        idx = jnp.min(jnp.where(valid & (v == mx), lane, big), axis=-1, keepdims=True)
        return mx, idx

    is_g = lane < N_GROUPS
    gmax, _ = first_argmax(logits, is_g)
    eg = jnp.where(is_g, jnp.exp(logits - gmax), 0.0)
    pg = eg / jnp.sum(eg, axis=-1, keepdims=True)
    pg_top, g_top = first_argmax(pg, is_g)
    e_lane0 = N_GROUPS + g_top * EXPERTS_PER_GROUP
    in_grp = (lane >= e_lane0) & (lane < e_lane0 + EXPERTS_PER_GROUP)
    w1, i1 = first_argmax(logits, in_grp)
    w2, i2 = first_argmax(logits, in_grp & (lane != i1))
    e2 = jnp.exp(w2 - w1)
    p1 = 1.0 / (1.0 + e2)
    p2 = e2 / (1.0 + e2)
    a = i1 - e_lane0
    b = i2 - e_lane0
    lo = jnp.minimum(a, b)
    hi = jnp.maximum(a, b)
    pair = lo * (7.0 - lo) * 0.5 + (hi - lo - 1.0)
    bucket = g_top * N_PAIRS + pair
    gate_lo = pg_top * jnp.where(a < b, p1, p2)
    gate_hi = pg_top * jnp.where(a < b, p2, p1)
    h_ref[:, d:] = jnp.where(lane == 0.0, bucket, jnp.where(lane == 1.0, gate_lo, jnp.where(lane == 2.0, gate_hi, 0.0)))


def _router(x, shift, scale, g, w_r, b_r, h_all, n_tok, row0):
    b, s, d = x.shape
    tm = _tile(s, 512)
    nt = s // tm
    assert row0 % tm == 0
    blk0 = row0 // tm
    r = shift.shape[1]
    rm = tm if r > 1 else 1
    mod_spec = pl.BlockSpec((1, rm, d), (lambda bb, i: (bb, i, 0)) if r > 1 else (lambda bb, i: (bb, 0, 0)))
    in_specs = [pl.BlockSpec((1, tm, d), lambda bb, i: (bb, i, 0)), mod_spec, mod_spec,
                pl.BlockSpec((1, d), lambda bb, i: (0, 0)),
                pl.BlockSpec((d, LANES), lambda bb, i: (0, 0)),
                pl.BlockSpec((1, LANES), lambda bb, i: (0, 0))]
    args = [x, shift, scale, g, w_r, b_r]
    aliases = {}
    if h_all is not None:
        in_specs += [pl.BlockSpec(memory_space=pl.ANY)]
        args += [h_all]
        aliases = {6: 0}
    return pl.pallas_call(
        _router_kernel,
        grid=(b, nt),
        in_specs=in_specs,
        out_specs=pl.BlockSpec((tm, d + LANES), lambda bb, i: (blk0 + bb * nt + i, 0)),
        out_shape=jax.ShapeDtypeStruct((n_tok, d + LANES), F32),
        input_output_aliases=aliases,
        compiler_params=_params("arbitrary", "arbitrary"),
        name="moe_router",
    )(*args)


def _expert_kernel(ea_ref, eb_ref, nval_ref, tok_ref, h_hbm, w1a_ref, w3a_ref, w2a_ref,
                   w1b_ref, w3b_ref, w2b_ref, y_hbm, hbuf, ybuf, gsem, ssem, *, tm, n_tok):
    t = pl.program_id(0)
    n_tiles = pl.num_programs(0)
    slot = lax.rem(t, 2)
    d = y_hbm.shape[1]

    def gather_start(tile, sl):
        def body(r, c):
            tok = jnp.minimum(tok_ref[tile * tm + r], n_tok - 1)
            pltpu.make_async_copy(h_hbm.at[pl.ds(tok, 1), :], hbuf.at[sl, pl.ds(r, 1), :], gsem.at[sl]).start()
            return c
        lax.fori_loop(0, tm, body, 0, unroll=ROW_DMA_UNROLL)

    def gather_wait(sl):
        pltpu.make_async_copy(h_hbm.at[pl.ds(0, tm), :], hbuf.at[sl], gsem.at[sl]).wait()

    def scatter_start(tile, sl):
        def body(r, c):
            tok = tok_ref[tile * tm + r]
            dst = jnp.where(tok < n_tok, tok, n_tok + sl * tm + r)
            pltpu.make_async_copy(ybuf.at[sl, pl.ds(r, 1), :], y_hbm.at[pl.ds(dst, 1), :], ssem.at[sl]).start()
            return c
        lax.fori_loop(0, tm, body, 0, unroll=ROW_DMA_UNROLL)

    def scatter_wait(sl):
        pltpu.make_async_copy(ybuf.at[sl], y_hbm.at[pl.ds(0, tm), :], ssem.at[sl]).wait()

    @pl.when(t == 0)
    def _():
        gather_start(0, 0)

    @pl.when(t + 1 < n_tiles)
    def _():
        gather_start(t + 1, 1 - slot)

    gather_wait(slot)
    used = nval_ref[t] > 0

    @pl.when((t >= 2) & (nval_ref[jnp.maximum(t - 2, 0)] > 0))
    def _():
        scatter_wait(slot)

    @pl.when(used)
    def _():
        rows = hbuf[slot]
        h = rows[:, :d].astype(BF16)
        gate = rows[:, d:]

        def act(w1_ref, w3_ref, gcol):
            a = jnp.dot(h, w1_ref[0, 0], preferred_element_type=F32)
            c = jnp.dot(h, w3_ref[0, 0], preferred_element_type=F32)
            return (a * jax.nn.sigmoid(a) * c * gate[:, gcol:gcol + 1]).astype(BF16)

        ybuf[slot] = (jnp.dot(act(w1a_ref, w3a_ref, 1), w2a_ref[0, 0], preferred_element_type=F32)
                      + jnp.dot(act(w1b_ref, w3b_ref, 2), w2b_ref[0, 0], preferred_element_type=F32))

        scatter_start(t, slot)

    @pl.when(t == n_tiles - 1)
    def _():
        @pl.when(used)
        def _():
            scatter_wait(slot)

        @pl.when((t >= 1) & (nval_ref[jnp.maximum(t - 1, 0)] > 0))
        def _():
            scatter_wait(1 - slot)


def _experts(h_all, tok_of_slot, tile_ea, tile_eb, tile_nval, w1_bf, w3_bf, w2_bf, layer, n_tok):
    width = h_all.shape[1]
    d = width - LANES
    tm = MXU_ROWS
    n_tiles = tile_ea.shape[0]
    f = w1_bf.shape[-1]
    amap = lambda t, ea, eb, nv, tok: (layer, ea[t], 0, 0)
    bmap = lambda t, ea, eb, nv, tok: (layer, eb[t], 0, 0)
    grid_spec = pltpu.PrefetchScalarGridSpec(
        num_scalar_prefetch=4,
        grid=(n_tiles,),
        in_specs=[pl.BlockSpec(memory_space=pl.ANY),
                  pl.BlockSpec((1, 1, d, f), amap), pl.BlockSpec((1, 1, d, f), amap), pl.BlockSpec((1, 1, f, d), amap),
                  pl.BlockSpec((1, 1, d, f), bmap), pl.BlockSpec((1, 1, d, f), bmap), pl.BlockSpec((1, 1, f, d), bmap)],
        out_specs=pl.BlockSpec(memory_space=pl.ANY),
        scratch_shapes=[pltpu.VMEM((2, tm, width), F32), pltpu.VMEM((2, tm, d), F32),
                        pltpu.SemaphoreType.DMA((2,)), pltpu.SemaphoreType.DMA((2,))],
    )
    return pl.pallas_call(
        functools.partial(_expert_kernel, tm=tm, n_tok=n_tok),
        grid_spec=grid_spec,
        out_shape=jax.ShapeDtypeStruct((n_tok + 2 * tm, d), F32),
        compiler_params=_params("arbitrary"),
        name="moe_experts",
    )(tile_ea, tile_eb, tile_nval, tok_of_slot, h_all, w1_bf, w3_bf, w2_bf, w1_bf, w3_bf, w2_bf)


def _schedule(bucket, n_tok, tm):
    n_tiles = (n_tok + N_BUCKETS * (tm - 1)) // tm + 1
    onehot = (bucket[:, None] == jnp.arange(N_BUCKETS, dtype=jnp.int32)[None, :]).astype(jnp.int32)
    csum = jnp.cumsum(onehot, axis=0)
    rank = jnp.sum(onehot * csum, axis=1) - 1
    counts = csum[-1]
    tiles_per = (counts + tm - 1) // tm
    tile_end = jnp.cumsum(tiles_per)
    tile_off = tile_end - tiles_per
    slot = (tile_off * tm)[bucket] + rank
    tok_of_slot = jnp.full((n_tiles * tm,), n_tok, jnp.int32).at[slot].set(jnp.arange(n_tok, dtype=jnp.int32))
    tile_ids = jnp.arange(n_tiles, dtype=jnp.int32)
    tile_bucket = jnp.sum((tile_ids[:, None] >= tile_end[None, :]).astype(jnp.int32), axis=1)
    used = tile_ids < tile_end[-1]
    last = jnp.maximum(tile_end[-1] - 1, 0)
    tile_bucket = jnp.where(used, tile_bucket, tile_bucket[last])
    tile_nval = jnp.where(used, jnp.clip(counts[tile_bucket] - (tile_ids - tile_off[tile_bucket]) * tm, 0, tm), 0)
    grp = tile_bucket // N_PAIRS
    pair = tile_bucket % N_PAIRS
    tile_ea = grp * EXPERTS_PER_GROUP + jnp.asarray(_PAIR_LO, jnp.int32)[pair]
    tile_eb = grp * EXPERTS_PER_GROUP + jnp.asarray(_PAIR_HI, jnp.int32)[pair]
    return tok_of_slot, tile_ea, tile_eb, tile_nval.astype(jnp.int32)


def _combine_kernel(x_ref, y_ref, gate_ref, xo_ref):
    xo_ref[0] = x_ref[0] + gate_ref[0] * y_ref[...]


def _combine(x, y_tok, gate, row0):
    b, s, d = x.shape
    tm = _tile(s, 512)
    nt = s // tm
    blk0 = row0 // tm
    r = gate.shape[1]
    rm = tm if r > 1 else 1
    gate_spec = pl.BlockSpec((1, rm, d), (lambda bb, i: (bb, i, 0)) if r > 1 else (lambda bb, i: (bb, 0, 0)))
    return pl.pallas_call(
        _combine_kernel,
        grid=(b, nt),
        in_specs=[pl.BlockSpec((1, tm, d), lambda bb, i: (bb, i, 0)),
                  pl.BlockSpec((tm, d), lambda bb, i: (blk0 + bb * nt + i, 0)),
                  gate_spec],
        out_specs=pl.BlockSpec((1, tm, d), lambda bb, i: (bb, i, 0)),
        out_shape=jax.ShapeDtypeStruct((b, s, d), F32),
        compiler_params=_params("arbitrary", "arbitrary"),
        name="moe_combine",
    )(x, y_tok, gate)


def _moe(xp, xs_flat, mods_p, mods_s, g2, w_r, b_r, w1_bf, w3_bf, w2_bf, layer):
    b, s, d = xp.shape
    ts = xs_flat.shape[1]
    n_p = b * s
    n_tok = n_p + ts
    h_all = _router(xp, mods_p[0], mods_p[1], g2, w_r, b_r, None, n_tok, 0)
    h_all = _router(xs_flat, mods_s[0], mods_s[1], g2, w_r, b_r, h_all, n_tok, n_p)
    bucket = h_all[:, d].astype(jnp.int32)
    tok_of_slot, tile_ea, tile_eb, tile_nval = _schedule(bucket, n_tok, MXU_ROWS)
    y_tok = _experts(h_all, tok_of_slot, tile_ea, tile_eb, tile_nval, w1_bf, w3_bf, w2_bf, layer, n_tok)
    return _combine(xp, y_tok, mods_p[2], 0), _combine(xs_flat, y_tok, mods_s[2], n_p)


def _rope_tables(pos):
    half = HEAD_DIM // 2
    inv = ROPE_THETA ** (-jnp.arange(half, dtype=F32) / half)
    ang = pos.astype(F32)[:, None] * inv[None, :]
    cos, sin = jnp.cos(ang), jnp.sin(ang)
    return jnp.concatenate([cos, cos], axis=-1), jnp.concatenate([-sin, sin], axis=-1)


def kernel(x_prompt, x_sample, state_pool, cache_k, cache_v, page_table, c_prompt, c_sample, w_ada, b_ada, g_norm1, g_norm2, w_pool, b_pool, pool_scale, w_qkv, g_q, g_k, lam_q1, lam_k1, lam_q2, lam_k2, g_sub, w_o, w_rg, b_rg, w_re, b_re, w1, w3, w2):
    bp, seq, d = x_prompt.shape
    bd, dec_t, _ = x_sample.shape
    depth = w_ada.shape[0]
    n_heads = cache_k.shape[3]
    qk_width = n_heads * 2 * HEAD_DIM
    page = cache_k.shape[2]
    past_len = page_table.shape[1] * page
    ts = bd * dec_t
    rows = max(8, 1 << (dec_t - 1).bit_length())
    pool_state = state_pool.shape[2]

    n_c = bp + bd
    c_all = jnp.concatenate([c_prompt, c_sample, jnp.zeros((-(-n_c // 8) * 8 - n_c, d), F32)], axis=0)
    mod = _ada(c_all, w_ada, b_ada)

    cos_p, sin_p = _rope_tables(jnp.arange(seq))
    cos_s, sin_s = _rope_tables(past_len + jnp.arange(dec_t))
    cos_s, sin_s = jnp.tile(cos_s, (bd, 1)), jnp.tile(sin_s, (bd, 1))
    cache_k2 = cache_k.reshape(cache_k.shape[0], cache_k.shape[1], page * n_heads * 2, HEAD_DIM)
    cache_v2 = cache_v.reshape(cache_v.shape[0], cache_v.shape[1], page * n_heads, V_DIM)
    n_attn = depth // 2
    k_nat_p = v_nat_p = k_nat_s = v_nat_s = None
    w1_bf, w3_bf, w2_bf = w1.astype(BF16), w3.astype(BF16), w2.astype(BF16)
    w_qkv_bf, w_o_bf = w_qkv.astype(BF16), w_o.astype(BF16)
    w_r = jnp.concatenate([w_rg, w_re, jnp.zeros((depth, d, LANES - w_rg.shape[-1] - w_re.shape[-1]), F32)], axis=-1)
    b_r = jnp.concatenate([b_rg, b_re, jnp.zeros((depth, LANES - b_rg.shape[-1] - b_re.shape[-1]), F32)], axis=-1)

    xp, xs = x_prompt, x_sample
    npool_p, npool_s = [], []
    for i in range(depth):
        chunks = [mod[i, :, c * d:(c + 1) * d] for c in range(6)]
        m_p = [c[:bp, None, :] for c in chunks]
        m_s = [c[bp:bp + bd, None, :] for c in chunks]
        m_s_rows = [jnp.repeat(c[bp:bp + bd], dec_t, axis=0)[None] for c in chunks]
        g1 = g_norm1[i][None]
        g2 = g_norm2[i][None]
        li = i // 2
        if i % 2 == 0:
            w_pool_bf = w_pool[li].astype(BF16)
            b_pool_i = b_pool[li].reshape(1, d)
            ls = pool_scale[li][None]
            xp, tail_p = _pool(xp, jnp.zeros((bp, HIST, d), F32), m_p[0], m_p[1], m_p[2], g1,
                               w_pool_bf, b_pool_i, ls, 0)
            npool_p.append(tail_p[:, HIST - pool_state:])
            hist_s = jnp.pad(state_pool[li], ((0, 0), (HIST - pool_state, 0), (0, 0)))
            xs_pad = jnp.pad(xs, ((0, 0), (0, rows - dec_t), (0, 0)))
            xs_new, tail_s = _pool(xs_pad, hist_s, m_s[0], m_s[1], m_s[2], g1, w_pool_bf, b_pool_i, ls, past_len)
            xs = xs_new[:, :dec_t]
            ext = jnp.concatenate([hist_s[:, :rows], tail_s], axis=1)
            npool_s.append(ext[:, HIST + dec_t - pool_state:HIST + dec_t])
        else:
            gq, gk, gs = g_q[li][None], g_k[li][None], g_sub[li][None]
            lam_vecs = jnp.stack([lam_q1[li], lam_k1[li], lam_q2[li], lam_k2[li]])
            q, k_bf, vt_bf, k_nat_p, v_nat_p = _qkv(xp, m_p[0], m_p[1], g1, w_qkv_bf, gq, gk, cos_p, sin_p, qk_width,
                                                    n_attn, li, k_nat_p, v_nat_p, True)
            o = _prompt_attn(q, k_bf, vt_bf, lam_vecs, g_sub[li][:, None], i)
            xp = _oproj(o, w_o_bf, li, xp, m_p[2])
            xs_flat = xs.reshape(1, ts, d)
            q, k_bf, v_bf, k_nat_s, v_nat_s = _qkv(xs_flat, m_s_rows[0], m_s_rows[1], g1, w_qkv_bf, gq, gk,
                                                   cos_s, sin_s, qk_width, n_attn, li, k_nat_s, v_nat_s, False)
            o = _decode_attn(q.reshape(bd, dec_t, n_heads, 2, HEAD_DIM), k_bf.reshape(bd, dec_t, n_heads, 2, HEAD_DIM),
                             v_bf.reshape(bd, dec_t, n_heads, V_DIM), cache_k2, cache_v2,
                             page_table, li, lam_vecs, gs, i, rows)
            xs = _oproj(o.reshape(1, ts, -1), w_o_bf, li, xs_flat, m_s_rows[2]).reshape(bd, dec_t, d)
        xp, xs_flat = _moe(xp, xs.reshape(1, ts, d), m_p[3:], m_s_rows[3:], g2, w_r[i], b_r[i][None],
                           w1_bf, w3_bf, w2_bf, i)
        xs = xs_flat.reshape(bd, dec_t, d)
    def v_out(v_nat, b, s):
        v = v_nat.reshape(n_attn, b, s, V_DIM // LANES, n_heads, LANES)
        return v.transpose(0, 1, 2, 4, 3, 5).reshape(n_attn, b, s, n_heads, V_DIM)

    return (xp, xs,
            k_nat_p.reshape(n_attn, bp, seq, n_heads, 2, HEAD_DIM), v_out(v_nat_p, bp, seq),
            jnp.stack(npool_p),
            k_nat_s.reshape(n_attn, bd, dec_t, n_heads, 2, HEAD_DIM), v_out(v_nat_s, bd, dec_t),
            jnp.stack(npool_s))
```

```python
import functools
import math

import jax
import jax.numpy as jnp
import numpy as np
from jax import lax
from jax.experimental import pallas as pl
from jax.experimental.pallas import tpu as pltpu

F32 = jnp.float32
BF16 = jnp.bfloat16

EPS = 1e-6
NEG = -1e30
ROPE_THETA = 10000.0
POOL_WINDOWS = (2, 4, 8, 16)
HIST = 16
HEAD_DIM = 128
V_DIM = 2 * HEAD_DIM
N_GROUPS = 4
EXPERTS_PER_GROUP = 4
N_PAIRS = 6
N_BUCKETS = N_GROUPS * N_PAIRS
LANES = 128
MXU_ROWS = 256
ROW_DMA_UNROLL = 8
DECODE_PAGES_PER_STEP = 8
VMEM_LIMIT = 48 * 1024 * 1024

_PAIR_LO = (0, 0, 0, 1, 1, 2)
_PAIR_HI = (1, 2, 3, 2, 3, 3)


def _lambda_init(layer):
    return 0.8 - 0.6 * math.exp(-0.3 * layer)


def _tile(n, pref):
    if n <= pref:
        return n
    t = pref
    while t >= 8:
        if n % t == 0 and t % 8 == 0:
            return t
        t -= 8
    return n


def _params(*sem):
    return pltpu.CompilerParams(dimension_semantics=sem, vmem_limit_bytes=VMEM_LIMIT)


def _modulate(x, g, shift, scale):
    ms = jnp.mean(x * x, axis=-1, keepdims=True)
    y = x * lax.rsqrt(ms + EPS) * g
    return y * (1.0 + scale) + shift


def _ada_kernel(c_ref, w_ref, b_ref, o_ref):
    c = c_ref[...]
    s = (c * jax.nn.sigmoid(c)).astype(BF16)
    o_ref[0] = jnp.dot(s, w_ref[0].astype(BF16), preferred_element_type=F32) + b_ref[0]


def _ada(c_all, w_ada, b_ada):
    depth, d, n = w_ada.shape
    r = c_all.shape[0]
    tn = _tile(n, 1024)
    return pl.pallas_call(
        _ada_kernel,
        grid=(depth, n // tn),
        in_specs=[pl.BlockSpec((r, d), lambda l, j: (0, 0)),
                  pl.BlockSpec((1, d, tn), lambda l, j: (l, 0, j)),
                  pl.BlockSpec((1, 1, tn), lambda l, j: (l, 0, j))],
        out_specs=pl.BlockSpec((1, r, tn), lambda l, j: (l, 0, j)),
        out_shape=jax.ShapeDtypeStruct((depth, r, n), F32),
        compiler_params=_params("arbitrary", "arbitrary"),
        name="ada",
    )(c_all, w_ada, b_ada.reshape(depth, 1, n))


def _pool_kernel(x_ref, hist_ref, shift_ref, scale_ref, gate_ref, g_ref, w_ref, b_ref, ls_ref,
                 xo_ref, tail_ref, hbuf, *, tm, pos0, n_tiles):
    j = pl.program_id(1)
    d = x_ref.shape[-1]
    grp = d // len(POOL_WINDOWS)

    @pl.when(j == 0)
    def _():
        hbuf[0:HIST, :] = hist_ref[0]

    x = x_ref[0]
    h = _modulate(x, g_ref[...], shift_ref[0], scale_ref[0])
    hbuf[HIST:HIST + tm, :] = h
    pos = pos0 + j * tm + lax.broadcasted_iota(jnp.int32, (tm, 1), 0)
    gate = gate_ref[0]
    for gi, w in enumerate(POOL_WINDOWS):
        lo = gi * grp
        s = hbuf[:, lo:lo + grp]
        k = 1
        while k < w:
            s = s + pltpu.roll(s, k, 0)
            k *= 2
        cnt = jnp.minimum(pos + 1, w).astype(F32)
        dlt = s[HIST:, :] / cnt - h[:, lo:lo + grp]
        y = jnp.dot(dlt.astype(BF16), w_ref[gi], preferred_element_type=F32) + b_ref[:, lo:lo + grp]
        y = y * ls_ref[:, lo:lo + grp]
        xo_ref[0, :, lo:lo + grp] = x[:, lo:lo + grp] + gate[:, lo:lo + grp] * y

    @pl.when(j == n_tiles - 1)
    def _():
        tail_ref[0] = hbuf[tm:tm + HIST, :]

    if n_tiles > 1:
        @pl.when(j < n_tiles - 1)
        def _():
            hbuf[0:HIST, :] = hbuf[tm:tm + HIST, :]


def _pool(x, hist, shift, scale, gate, g, w_pool_bf, b_pool, ls, pos0):
    b, s, d = x.shape
    tm = _tile(s, 512)
    n_tiles = s // tm
    assert n_tiles == 1 or tm >= HIST
    ng, grp, _ = w_pool_bf.shape
    row = lambda bb, j: (bb, 0, 0)
    return pl.pallas_call(
        functools.partial(_pool_kernel, tm=tm, pos0=pos0, n_tiles=n_tiles),
        grid=(b, n_tiles),
        in_specs=[pl.BlockSpec((1, tm, d), lambda bb, j: (bb, j, 0)),
                  pl.BlockSpec((1, HIST, d), row),
                  pl.BlockSpec((1, 1, d), row), pl.BlockSpec((1, 1, d), row), pl.BlockSpec((1, 1, d), row),
                  pl.BlockSpec((1, d), lambda bb, j: (0, 0)),
                  pl.BlockSpec((ng, grp, grp), lambda bb, j: (0, 0, 0)),
                  pl.BlockSpec((1, d), lambda bb, j: (0, 0)),
                  pl.BlockSpec((1, d), lambda bb, j: (0, 0))],
        out_specs=[pl.BlockSpec((1, tm, d), lambda bb, j: (bb, j, 0)),
                   pl.BlockSpec((1, HIST, d), row)],
        out_shape=[jax.ShapeDtypeStruct((b, s, d), F32), jax.ShapeDtypeStruct((b, HIST, d), F32)],
        scratch_shapes=[pltpu.VMEM((HIST + tm, d), F32)],
        compiler_params=_params("arbitrary", "arbitrary"),
        name="pool_mixer",
    )(x, hist, shift, scale, gate, g, w_pool_bf, b_pool, ls)


def _qkv_kernel(x_ref, shift_ref, scale_ref, g_ref, w_ref, gq_ref, gk_ref, cos_ref, sin_ref, *rest,
                nq, nk, nv, q_scale, n_alias, v_t):
    q_ref, kb_ref, vb_ref, k_ref, v_ref, h_scr = rest[n_alias:]
    n = pl.program_id(2)
    tm = x_ref.shape[1]
    tn = w_ref.shape[2]
    cpb = tn // HEAD_DIM
    hpb = tn // V_DIM
    n_chunks = nk * cpb
    n_heads = nv * hpb

    @pl.when(n == 0)
    def _():
        h_scr[...] = _modulate(x_ref[0], g_ref[...], shift_ref[0], scale_ref[0]).astype(BF16)

    acc = jnp.dot(h_scr[...], w_ref[0], preferred_element_type=F32)

    def norm_rope(c, g):
        xc = acc[:, c * HEAD_DIM:(c + 1) * HEAD_DIM]
        y = xc * lax.rsqrt(jnp.mean(xc * xc, axis=-1, keepdims=True) + EPS) * g
        return y * cos_ref[...] + pltpu.roll(y, HEAD_DIM // 2, 1) * sin_ref[...]

    @pl.when(n < nq)
    def _():
        for c in range(cpb):
            q_ref[0, :, c * HEAD_DIM:(c + 1) * HEAD_DIM] = (norm_rope(c, gq_ref[...]) * q_scale).astype(BF16)

    for kn in range(nk):
        @pl.when(n == nq + kn)
        def _(kn=kn):
            for c in range(cpb):
                y = norm_rope(c, gk_ref[...])
                kb_ref[0, :, c * HEAD_DIM:(c + 1) * HEAD_DIM] = y.astype(BF16)
                k_ref[0, 0, pl.ds(kn * cpb + c, tm, stride=n_chunks), :] = y

    for vn in range(nv):
        @pl.when(n == nq + nk + vn)
        def _(vn=vn):
            if v_t:
                for c in range(hpb):
                    vb_ref[0, c] = acc[:, c * V_DIM:(c + 1) * V_DIM].T.astype(BF16)
            else:
                vb_ref[0] = acc.astype(BF16)
            for c in range(hpb):
                for half in range(V_DIM // LANES):
                    lo = c * V_DIM + half * LANES
                    v_ref[0, 0, pl.ds(half * n_heads + vn * hpb + c, tm, stride=2 * n_heads), :] = acc[:, lo:lo + LANES]


def _qkv(x, shift, scale, g, w_qkv_bf, g_q, g_k, cos, sin, qk_width, n_layers, lidx, k_nat, v_nat, v_t):
    b, s, d = x.shape
    n_all = w_qkv_bf.shape[2]
    v_width = n_all - 2 * qk_width
    tm = _tile(s, 512)
    tn = 512 if (qk_width % 512 == 0 and v_width % 512 == 0) else V_DIM
    nq = nk = qk_width // tn
    nv = v_width // tn
    n_chunks = qk_width // HEAD_DIM
    n_heads = v_width // V_DIM
    r = shift.shape[1]
    rm = tm if r > 1 else 1
    mod_spec = pl.BlockSpec((1, rm, d), (lambda bb, i, n: (bb, i, 0)) if r > 1 else (lambda bb, i, n: (bb, 0, 0)))
    qmap = lambda bb, i, n: (bb, i, jnp.minimum(n, nq - 1))
    kmap = lambda bb, i, n: (bb, i, jnp.clip(n - nq, 0, nk - 1))
    vmap = lambda bb, i, n: (bb, i, jnp.clip(n - nq - nk, 0, nv - 1))
    nat_map = lambda bb, i, n: (lidx, bb, i, 0)
    tab_spec = pl.BlockSpec((tm, HEAD_DIM), lambda bb, i, n: (i, 0))
    vec_spec = pl.BlockSpec((1, HEAD_DIM), lambda bb, i, n: (0, 0))
    in_specs = [pl.BlockSpec((1, tm, d), lambda bb, i, n: (bb, i, 0)),
                mod_spec, mod_spec,
                pl.BlockSpec((1, d), lambda bb, i, n: (0, 0)),
                pl.BlockSpec((1, d, tn), lambda bb, i, n: (lidx, 0, n)),
                vec_spec, vec_spec, tab_spec, tab_spec]
    args = [x, shift, scale, g, w_qkv_bf, g_q, g_k, cos, sin]
    aliases = {}
    if k_nat is not None:
        in_specs += [pl.BlockSpec(memory_space=pl.ANY), pl.BlockSpec(memory_space=pl.ANY)]
        args += [k_nat, v_nat]
        aliases = {9: 3, 10: 4}
    if v_t:
        hpb = tn // V_DIM
        vb_spec = pl.BlockSpec((1, hpb, V_DIM, tm), lambda bb, i, n: (bb, jnp.clip(n - nq - nk, 0, nv - 1), 0, i))
        vb_shape = jax.ShapeDtypeStruct((b, n_heads, V_DIM, s), BF16)
    else:
        vb_spec = pl.BlockSpec((1, tm, tn), vmap)
        vb_shape = jax.ShapeDtypeStruct((b, s, v_width), BF16)
    return pl.pallas_call(
        functools.partial(_qkv_kernel, nq=nq, nk=nk, nv=nv, q_scale=HEAD_DIM ** -0.5 * math.log2(math.e),
                          n_alias=len(aliases), v_t=v_t),
        grid=(b, s // tm, nq + nk + nv),
        in_specs=in_specs,
        out_specs=[pl.BlockSpec((1, tm, tn), qmap), pl.BlockSpec((1, tm, tn), kmap), vb_spec,
                   pl.BlockSpec((1, 1, tm * n_chunks, HEAD_DIM), nat_map),
                   pl.BlockSpec((1, 1, tm * 2 * n_heads, LANES), nat_map)],
        out_shape=[jax.ShapeDtypeStruct((b, s, qk_width), BF16), jax.ShapeDtypeStruct((b, s, qk_width), BF16),
                   vb_shape,
                   jax.ShapeDtypeStruct((n_layers, b, s * n_chunks, HEAD_DIM), F32),
                   jax.ShapeDtypeStruct((n_layers, b, s * 2 * n_heads, LANES), F32)],
        scratch_shapes=[pltpu.VMEM((tm, d), BF16)],
        input_output_aliases=aliases,
        compiler_params=_params("arbitrary", "arbitrary", "arbitrary"),
        name="qkv_proj",
    )(*args)


def _lam_value(lam_ref, lam_init):
    l = lam_ref[...]
    a = jnp.sum(l[0:1] * l[1:2], axis=-1, keepdims=True)
    b = jnp.sum(l[2:3] * l[3:4], axis=-1, keepdims=True)
    return jnp.exp(a) - jnp.exp(b) + lam_init


def _softmax_step(s_blocks, v_blocks, m_prev, l_prev, acc_prev):
    m_new = m_prev
    for s in s_blocks:
        m_new = jnp.maximum(m_new, jnp.max(s, axis=-1, keepdims=True))
    alpha = jnp.exp2(m_prev - m_new)
    l_new = alpha * l_prev
    acc_new = alpha * acc_prev
    for s, v_bf in zip(s_blocks, v_blocks):
        p = jnp.exp2(s - m_new)
        l_new = l_new + jnp.sum(p, axis=-1, keepdims=True)
        acc_new = acc_new + jnp.dot(p.astype(BF16), v_bf, preferred_element_type=F32)
    return m_new, l_new, acc_new


def _head_out(o0, o1, lam, gsub, out_scale):
    o = o0 - lam * o1
    return o * lax.rsqrt(jnp.mean(o * o, axis=-1, keepdims=True) + EPS) * gsub * out_scale


def _attn_kernel(qi_ref, kj_ref, q_ref, k_ref, vt_ref, lam_ref, gsub_ref, o_ref, m_scr, l_scr, acc_scr,
                 *, lam_init, out_scale):
    t = pl.program_id(2)
    i = qi_ref[t]
    j = kj_ref[t]

    @pl.when(j == 0)
    def _():
        m_scr[...] = jnp.full(m_scr.shape, NEG, F32)
        l_scr[...] = jnp.zeros(l_scr.shape, F32)
        acc_scr[...] = jnp.zeros(acc_scr.shape, F32)

    nt = (((1,), (1,)), ((), ()))

    def step(causal):
        vt = vt_ref[0, 0]
        for mp in range(2):
            cols = slice(mp * HEAD_DIM, (mp + 1) * HEAD_DIM)
            s = lax.dot_general(k_ref[0, :, cols], q_ref[0, :, cols], nt, preferred_element_type=F32)
            if causal:
                key = lax.broadcasted_iota(jnp.int32, s.shape, 0)
                qry = lax.broadcasted_iota(jnp.int32, s.shape, 1)
                s = jnp.where(key <= qry, s, NEG)
            m_prev = m_scr[mp]
            m_new = jnp.maximum(m_prev, jnp.max(s, axis=0, keepdims=True))
            alpha = jnp.exp2(m_prev - m_new)
            p = jnp.exp2(s - m_new)
            l_scr[mp] = alpha * l_scr[mp] + jnp.sum(p, axis=0, keepdims=True)
            acc_scr[mp] = alpha * acc_scr[mp] + jnp.dot(vt, p.astype(BF16), preferred_element_type=F32)
            m_scr[mp] = m_new

    @pl.when(j < i)
    def _():
        step(False)

    @pl.when(j == i)
    def _():
        step(True)
        lam = _lam_value(lam_ref, lam_init)
        o = acc_scr[0] / l_scr[0] - lam * (acc_scr[1] / l_scr[1])
        o = o * lax.rsqrt(jnp.mean(o * o, axis=0, keepdims=True) + EPS) * gsub_ref[...] * out_scale
        o_ref[0] = o.T.astype(BF16)


def _prompt_attn(q_bf, k_bf, vt_bf, lam_vecs, g_sub_col, layer):
    b, s, qk = q_bf.shape
    h = qk // (2 * HEAD_DIM)
    tq = _tile(s, 512)
    nb = s // tq
    qi = np.array([i for i in range(nb) for _ in range(i + 1)], np.int32)
    kj = np.array([j for i in range(nb) for j in range(i + 1)], np.int32)
    grid_spec = pltpu.PrefetchScalarGridSpec(
        num_scalar_prefetch=2,
        grid=(b, h, len(qi)),
        in_specs=[pl.BlockSpec((1, tq, 2 * HEAD_DIM), lambda bb, hh, t, qi_r, kj_r: (bb, qi_r[t], hh)),
                  pl.BlockSpec((1, tq, 2 * HEAD_DIM), lambda bb, hh, t, qi_r, kj_r: (bb, kj_r[t], hh)),
                  pl.BlockSpec((1, 1, V_DIM, tq), lambda bb, hh, t, qi_r, kj_r: (bb, hh, 0, kj_r[t])),
                  pl.BlockSpec((4, HEAD_DIM), lambda bb, hh, t, qi_r, kj_r: (0, 0)),
                  pl.BlockSpec((V_DIM, 1), lambda bb, hh, t, qi_r, kj_r: (0, 0))],
        out_specs=pl.BlockSpec((1, tq, V_DIM), lambda bb, hh, t, qi_r, kj_r: (bb, qi_r[t], hh)),
        scratch_shapes=[pltpu.VMEM((2, 1, tq), F32), pltpu.VMEM((2, 1, tq), F32),
                        pltpu.VMEM((2, V_DIM, tq), F32)],
    )
    return pl.pallas_call(
        functools.partial(_attn_kernel, lam_init=_lambda_init(layer), out_scale=1.0 - _lambda_init(layer)),
        grid_spec=grid_spec,
        out_shape=jax.ShapeDtypeStruct((b, s, h * V_DIM), BF16),
        compiler_params=_params("arbitrary", "arbitrary", "arbitrary"),
        name="prompt_attn",
    )(jnp.asarray(qi), jnp.asarray(kj), q_bf, k_bf, vt_bf, lam_vecs, g_sub_col)


def _decode_kernel(pt_ref, q_ref, kn_ref, vn_ref, *rest, ppb, n_heads, rows, t_new, lam_init, out_scale):
    kc_refs, vc_refs = rest[:ppb], rest[ppb:2 * ppb]
    lam_ref, gsub_ref, o_ref, bias_scr, m_scr, l_scr, acc_scr = rest[2 * ppb:]
    j = pl.program_id(1)
    n_steps = pl.num_programs(1)
    hq = n_heads * rows
    log_rows = rows.bit_length() - 1
    log_heads = n_heads.bit_length() - 1
    nt = (((1,), (1,)), ((), ()))

    def row_head(shape):
        r = lax.broadcasted_iota(jnp.int32, shape, 0)
        return jnp.right_shift(jnp.bitwise_and(r, hq - 1), log_rows)

    @pl.when(j == 0)
    def _():
        c = lax.broadcasted_iota(jnp.int32, bias_scr.shape, 1)
        bias_scr[...] = jnp.where(row_head(bias_scr.shape) == jnp.bitwise_and(c, n_heads - 1), 0.0, NEG)
        m_scr[...] = jnp.full(m_scr.shape, NEG, F32)
        l_scr[...] = jnp.zeros(l_scr.shape, F32)
        acc_scr[...] = jnp.zeros(acc_scr.shape, F32)

    def scores(k0, k1):
        q = q_ref[0]
        return jnp.concatenate([lax.dot_general(q[:hq], k0, nt, preferred_element_type=F32),
                                lax.dot_general(q[hq:], k1, nt, preferred_element_type=F32)], axis=0)

    def step(s_blocks, v_blocks):
        m, l, acc = _softmax_step(s_blocks, v_blocks, m_scr[...], l_scr[...], acc_scr[...])
        m_scr[...] = m
        l_scr[...] = l
        acc_scr[...] = acc

    half = kc_refs[0].shape[2] // 2
    s_blocks = []
    for kc_ref in kc_refs:
        k0 = kc_ref[0, 0, pl.ds(0, half, stride=2), :].astype(BF16)
        k1 = kc_ref[0, 0, pl.ds(1, half, stride=2), :].astype(BF16)
        s_blocks.append(scores(k0, k1) + bias_scr[...])
    step(s_blocks, [vc_ref[0, 0].astype(BF16) for vc_ref in vc_refs])

    @pl.when(j == n_steps - 1)
    def _():
        s_new = scores(kn_ref[0, 0], kn_ref[0, 1])
        r = lax.broadcasted_iota(jnp.int32, s_new.shape, 0)
        c = lax.broadcasted_iota(jnp.int32, s_new.shape, 1)
        key = jnp.right_shift(c, log_heads)
        ok = ((row_head(s_new.shape) == jnp.bitwise_and(c, n_heads - 1))
              & (key <= jnp.bitwise_and(r, rows - 1)) & (key < t_new))
        step([jnp.where(ok, s_new, NEG)], [vn_ref[0]])
        lam = _lam_value(lam_ref, lam_init)
        acc = acc_scr[...]
        l = l_scr[...]
        o_ref[0] = _head_out(acc[:hq] / l[:hq], acc[hq:] / l[hq:], lam, gsub_ref[...], out_scale)


def _decode_attn(q_bf, k_bf, v_bf, cache_k, cache_v, page_table, li, lam_vecs, g_sub, layer, rows):
    bd, t_new, h = q_bf.shape[:3]
    assert rows & (rows - 1) == 0 and h & (h - 1) == 0 and t_new <= rows
    keys = max(rows, LANES // h)
    n_pages = page_table.shape[1]
    pad_t = lambda a, n: jnp.pad(a, ((0, 0), (0, n - t_new)) + ((0, 0),) * (a.ndim - 2))
    q2 = pad_t(q_bf, rows).transpose(0, 3, 2, 1, 4).reshape(bd, 2 * h * rows, HEAD_DIM)
    kn = pad_t(k_bf, keys).transpose(0, 3, 1, 2, 4).reshape(bd, 2, keys * h, HEAD_DIM)
    vn = pad_t(v_bf, keys).reshape(bd, keys * h, V_DIM)
    ppb = max(p for p in range(1, DECODE_PAGES_PER_STEP + 1) if n_pages % p == 0)
    cmaps = [(lambda bb, j, pt, u=u: (li, pt[bb * n_pages + j * ppb + u], 0, 0)) for u in range(ppb)]
    grid_spec = pltpu.PrefetchScalarGridSpec(
        num_scalar_prefetch=1,
        grid=(bd, n_pages // ppb),
        in_specs=[pl.BlockSpec((1, 2 * h * rows, HEAD_DIM), lambda bb, j, pt: (bb, 0, 0)),
                  pl.BlockSpec((1, 2, keys * h, HEAD_DIM), lambda bb, j, pt: (bb, 0, 0, 0)),
                  pl.BlockSpec((1, keys * h, V_DIM), lambda bb, j, pt: (bb, 0, 0))]
                 + [pl.BlockSpec((1, 1) + cache_k.shape[2:], cm) for cm in cmaps]
                 + [pl.BlockSpec((1, 1) + cache_v.shape[2:], cm) for cm in cmaps]
                 + [pl.BlockSpec((4, HEAD_DIM), lambda bb, j, pt: (0, 0)),
                    pl.BlockSpec((1, V_DIM), lambda bb, j, pt: (0, 0))],
        out_specs=pl.BlockSpec((1, h * rows, V_DIM), lambda bb, j, pt: (bb, 0, 0)),
        scratch_shapes=[pltpu.VMEM((2 * h * rows, cache_v.shape[2]), F32),
                        pltpu.VMEM((2 * h * rows, 1), F32), pltpu.VMEM((2 * h * rows, 1), F32),
                        pltpu.VMEM((2 * h * rows, V_DIM), F32)],
    )
    o = pl.pallas_call(
        functools.partial(_decode_kernel, ppb=ppb, n_heads=h, rows=rows, t_new=t_new, lam_init=_lambda_init(layer),
                          out_scale=1.0 - _lambda_init(layer)),
        grid_spec=grid_spec,
        out_shape=jax.ShapeDtypeStruct((bd, h * rows, V_DIM), F32),
        compiler_params=_params("arbitrary", "arbitrary"),
        name="decode_attn",
    )(page_table.reshape(-1), q2, kn, vn, *([cache_k] * ppb), *([cache_v] * ppb), lam_vecs, g_sub)
    o = o.reshape(bd, h, rows, V_DIM).transpose(0, 2, 1, 3)[:, :t_new]
    return o.reshape(bd, t_new, h * V_DIM).astype(BF16)


def _oproj_kernel(o_ref, w_ref, x_ref, gate_ref, xo_ref):
    y = jnp.dot(o_ref[0], w_ref[0], preferred_element_type=F32)
    xo_ref[0] = x_ref[0] + gate_ref[0] * y


def _oproj(o_bf, w_o_bf, li, x, gate):
    b, s, d = x.shape
    kdim = o_bf.shape[-1]
    tm = _tile(s, 512)
    r = gate.shape[1]
    rm = tm if r > 1 else 1
    gate_spec = pl.BlockSpec((1, rm, d), (lambda bb, i: (bb, i, 0)) if r > 1 else (lambda bb, i: (bb, 0, 0)))
    return pl.pallas_call(
        _oproj_kernel,
        grid=(b, s // tm),
        in_specs=[pl.BlockSpec((1, tm, kdim), lambda bb, i: (bb, i, 0)),
                  pl.BlockSpec((1, kdim, d), lambda bb, i: (li, 0, 0)),
                  pl.BlockSpec((1, tm, d), lambda bb, i: (bb, i, 0)),
                  gate_spec],
        out_specs=pl.BlockSpec((1, tm, d), lambda bb, i: (bb, i, 0)),
        out_shape=jax.ShapeDtypeStruct((b, s, d), F32),
        compiler_params=_params("arbitrary", "arbitrary"),
        name="out_proj",
    )(o_bf, w_o_bf, x, gate)


def _split_bf16(a):
    hi = a.astype(BF16)
    lo = (a - hi.astype(F32)).astype(BF16)
    return hi, lo


def _router_kernel(x_ref, shift_ref, scale_ref, g_ref, w_ref, b_ref, *rest):
    h_ref = rest[-1]
    d = x_ref.shape[-1]
    h = _modulate(x_ref[0], g_ref[...], shift_ref[0], scale_ref[0])
    h_hi, h_lo = _split_bf16(h)
    h_ref[:, :d] = h
    w_hi, w_lo = _split_bf16(w_ref[...])
    logits = (jnp.dot(h_hi, w_hi, preferred_element_type=F32)
              + (jnp.dot(h_hi, w_lo, preferred_element_type=F32)
                 + jnp.dot(h_lo, w_hi, preferred_element_type=F32))) + b_ref[...]
    lane = lax.broadcasted_iota(jnp.int32, logits.shape, 1).astype(F32)
    big = float(LANES)

    def first_argmax(vals, valid):
        v = jnp.where(valid, vals, -jnp.inf)
        mx = jnp.max(v, axis=-1, keepdims=True)
        idx = jnp.min(jnp.where(valid & (v == mx), lane, big), axis=-1, keepdims=True)
        return mx, idx

    is_g = lane < N_GROUPS
    gmax, _ = first_argmax(logits, is_g)
    eg = jnp.where(is_g, jnp.exp(logits - gmax), 0.0)
    pg = eg / jnp.sum(eg, axis=-1, keepdims=True)
    pg_top, g_top = first_argmax(pg, is_g)
    e_lane0 = N_GROUPS + g_top * EXPERTS_PER_GROUP
    in_grp = (lane >= e_lane0) & (lane < e_lane0 + EXPERTS_PER_GROUP)
    w1, i1 = first_argmax(logits, in_grp)
    w2, i2 = first_argmax(logits, in_grp & (lane != i1))
    e2 = jnp.exp(w2 - w1)
    p1 = 1.0 / (1.0 + e2)
    p2 = e2 / (1.0 + e2)
    a = i1 - e_lane0
    b = i2 - e_lane0
    lo = jnp.minimum(a, b)
    hi = jnp.maximum(a, b)
    pair = lo * (7.0 - lo) * 0.5 + (hi - lo - 1.0)
    bucket = g_top * N_PAIRS + pair
    gate_lo = pg_top * jnp.where(a < b, p1, p2)
    gate_hi = pg_top * jnp.where(a < b, p2, p1)
    h_ref[:, d:] = jnp.where(lane == 0.0, bucket, jnp.where(lane == 1.0, gate_lo, jnp.where(lane == 2.0, gate_hi, 0.0)))


def _router(x, shift, scale, g, w_r, b_r, h_all, n_tok, row0):
    b, s, d = x.shape
    tm = _tile(s, 512)
    nt = s // tm
    assert row0 % tm == 0
    blk0 = row0 // tm
    r = shift.shape[1]
    rm = tm if r > 1 else 1
    mod_spec = pl.BlockSpec((1, rm, d), (lambda bb, i: (bb, i, 0)) if r > 1 else (lambda bb, i: (bb, 0, 0)))
    in_specs = [pl.BlockSpec((1, tm, d), lambda bb, i: (bb, i, 0)), mod_spec, mod_spec,
                pl.BlockSpec((1, d), lambda bb, i: (0, 0)),
                pl.BlockSpec((d, LANES), lambda bb, i: (0, 0)),
                pl.BlockSpec((1, LANES), lambda bb, i: (0, 0))]
    args = [x, shift, scale, g, w_r, b_r]
    aliases = {}
    if h_all is not None:
        in_specs += [pl.BlockSpec(memory_space=pl.ANY)]
        args += [h_all]
        aliases = {6: 0}
    return pl.pallas_call(
        _router_kernel,
        grid=(b, nt),
        in_specs=in_specs,
        out_specs=pl.BlockSpec((tm, d + LANES), lambda bb, i: (blk0 + bb * nt + i, 0)),
        out_shape=jax.ShapeDtypeStruct((n_tok, d + LANES), F32),
        input_output_aliases=aliases,
        compiler_params=_params("arbitrary", "arbitrary"),
        name="moe_router",
    )(*args)


def _expert_kernel(ea_ref, eb_ref, nval_ref, tok_ref, h_hbm, w1a_ref, w3a_ref, w2a_ref,
                   w1b_ref, w3b_ref, w2b_ref, y_hbm, hbuf, ybuf, gsem, ssem, *, tm, n_tok):
    t = pl.program_id(0)
    n_tiles = pl.num_programs(0)
    slot = lax.rem(t, 2)
    d = y_hbm.shape[1]

    def row_loop(issue):
        def body(g, c):
            for u in range(ROW_DMA_UNROLL):
                issue(g * ROW_DMA_UNROLL + u, u % 2)
            return c
        lax.fori_loop(0, tm // ROW_DMA_UNROLL, body, 0)

    def gather_start(tile, sl):
        def issue(r, prio):
            tok = jnp.minimum(tok_ref[tile * tm + r], n_tok - 1)
            pltpu.make_async_copy(h_hbm.at[pl.ds(tok, 1), :], hbuf.at[sl, pl.ds(r, 1), :],
                                  gsem.at[sl]).start(priority=prio)
        row_loop(issue)

    def gather_wait(sl):
        pltpu.make_async_copy(h_hbm.at[pl.ds(0, tm), :], hbuf.at[sl], gsem.at[sl]).wait()

    def scatter_start(tile, sl):
        def issue(r, prio):
            tok = tok_ref[tile * tm + r]
            dst = jnp.where(tok < n_tok, tok, n_tok + sl * tm + r)
            pltpu.make_async_copy(ybuf.at[sl, pl.ds(r, 1), :], y_hbm.at[pl.ds(dst, 1), :],
                                  ssem.at[sl]).start(priority=prio)
        row_loop(issue)

    def scatter_wait(sl):
        pltpu.make_async_copy(ybuf.at[sl], y_hbm.at[pl.ds(0, tm), :], ssem.at[sl]).wait()

    used = nval_ref[t] > 0

    @pl.when((t == 0) & used)
    def _():
        gather_start(0, 0)

    @pl.when((t + 1 < n_tiles) & (nval_ref[jnp.minimum(t + 1, n_tiles - 1)] > 0))
    def _():
        gather_start(t + 1, 1 - slot)

    @pl.when(used)
    def _():
        gather_wait(slot)

    @pl.when((t >= 2) & (nval_ref[jnp.maximum(t - 2, 0)] > 0))
    def _():
        scatter_wait(slot)

    @pl.when(used)
    def _():
        rows = hbuf[slot]
        h = rows[:, :d].astype(BF16)
        gate = rows[:, d:]

        def act(w1_ref, w3_ref, gcol):
            a = jnp.dot(h, w1_ref[0, 0], preferred_element_type=F32)
            c = jnp.dot(h, w3_ref[0, 0], preferred_element_type=F32)
            return (a * jax.nn.sigmoid(a) * c * gate[:, gcol:gcol + 1]).astype(BF16)

        ybuf[slot] = (jnp.dot(act(w1a_ref, w3a_ref, 1), w2a_ref[0, 0], preferred_element_type=F32)
                      + jnp.dot(act(w1b_ref, w3b_ref, 2), w2b_ref[0, 0], preferred_element_type=F32))

        scatter_start(t, slot)

    @pl.when(t == n_tiles - 1)
    def _():
        @pl.when(used)
        def _():
            scatter_wait(slot)

        @pl.when((t >= 1) & (nval_ref[jnp.maximum(t - 1, 0)] > 0))
        def _():
            scatter_wait(1 - slot)


def _experts(h_all, tok_of_slot, tile_ea, tile_eb, tile_nval, w1_bf, w3_bf, w2_bf, layer, n_tok):
    width = h_all.shape[1]
    d = width - LANES
    tm = MXU_ROWS
    n_tiles = tile_ea.shape[0]
    f = w1_bf.shape[-1]
    amap = lambda t, ea, eb, nv, tok: (layer, ea[t], 0, 0)
    bmap = lambda t, ea, eb, nv, tok: (layer, eb[t], 0, 0)
    grid_spec = pltpu.PrefetchScalarGridSpec(
        num_scalar_prefetch=4,
        grid=(n_tiles,),
        in_specs=[pl.BlockSpec(memory_space=pl.ANY),
                  pl.BlockSpec((1, 1, d, f), amap), pl.BlockSpec((1, 1, d, f), amap), pl.BlockSpec((1, 1, f, d), amap),
                  pl.BlockSpec((1, 1, d, f), bmap), pl.BlockSpec((1, 1, d, f), bmap), pl.BlockSpec((1, 1, f, d), bmap)],
        out_specs=pl.BlockSpec(memory_space=pl.ANY),
        scratch_shapes=[pltpu.VMEM((2, tm, width), F32), pltpu.VMEM((2, tm, d), F32),
                        pltpu.SemaphoreType.DMA((2,)), pltpu.SemaphoreType.DMA((2,))],
    )
    return pl.pallas_call(
        functools.partial(_expert_kernel, tm=tm, n_tok=n_tok),
        grid_spec=grid_spec,
        out_shape=jax.ShapeDtypeStruct((n_tok + 2 * tm, d), F32),
        compiler_params=_params("arbitrary"),
        name="moe_experts",
    )(tile_ea, tile_eb, tile_nval, tok_of_slot, h_all, w1_bf, w3_bf, w2_bf, w1_bf, w3_bf, w2_bf)


def _schedule(bucket, n_tok, tm):
    n_tiles = (n_tok + N_BUCKETS * (tm - 1)) // tm + 1
    onehot = (bucket[:, None] == jnp.arange(N_BUCKETS, dtype=jnp.int32)[None, :]).astype(jnp.int32)
    csum = jnp.cumsum(onehot, axis=0)
    rank = jnp.sum(onehot * csum, axis=1) - 1
    counts = csum[-1]
    tiles_per = (counts + tm - 1) // tm
    tile_end = jnp.cumsum(tiles_per)
    tile_off = tile_end - tiles_per
    slot = (tile_off * tm)[bucket] + rank
    tok_of_slot = jnp.full((n_tiles * tm,), n_tok, jnp.int32).at[slot].set(jnp.arange(n_tok, dtype=jnp.int32))
    tile_ids = jnp.arange(n_tiles, dtype=jnp.int32)
    tile_bucket = jnp.sum((tile_ids[:, None] >= tile_end[None, :]).astype(jnp.int32), axis=1)
    used = tile_ids < tile_end[-1]
    last = jnp.maximum(tile_end[-1] - 1, 0)
    tile_bucket = jnp.where(used, tile_bucket, tile_bucket[last])
    tile_nval = jnp.where(used, jnp.clip(counts[tile_bucket] - (tile_ids - tile_off[tile_bucket]) * tm, 0, tm), 0)
    grp = tile_bucket // N_PAIRS
    pair = tile_bucket % N_PAIRS
    tile_ea = grp * EXPERTS_PER_GROUP + jnp.asarray(_PAIR_LO, jnp.int32)[pair]
    tile_eb = grp * EXPERTS_PER_GROUP + jnp.asarray(_PAIR_HI, jnp.int32)[pair]
    return tok_of_slot, tile_ea, tile_eb, tile_nval.astype(jnp.int32)


def _combine_kernel(x_ref, y_ref, gate_ref, xo_ref):
    xo_ref[0] = x_ref[0] + gate_ref[0] * y_ref[...]


def _combine(x, y_tok, gate, row0):
    b, s, d = x.shape
    tm = _tile(s, 512)
    nt = s // tm
    blk0 = row0 // tm
    r = gate.shape[1]
    rm = tm if r > 1 else 1
    gate_spec = pl.BlockSpec((1, rm, d), (lambda bb, i: (bb, i, 0)) if r > 1 else (lambda bb, i: (bb, 0, 0)))
    return pl.pallas_call(
        _combine_kernel,
        grid=(b, nt),
        in_specs=[pl.BlockSpec((1, tm, d), lambda bb, i: (bb, i, 0)),
                  pl.BlockSpec((tm, d), lambda bb, i: (blk0 + bb * nt + i, 0)),
                  gate_spec],
        out_specs=pl.BlockSpec((1, tm, d), lambda bb, i: (bb, i, 0)),
        out_shape=jax.ShapeDtypeStruct((b, s, d), F32),
        compiler_params=_params("arbitrary", "arbitrary"),
        name="moe_combine",
    )(x, y_tok, gate)


def _moe(xp, xs_flat, mods_p, mods_s, g2, w_r, b_r, w1_bf, w3_bf, w2_bf, layer):
    b, s, d = xp.shape
    ts = xs_flat.shape[1]
    n_p = b * s
    n_tok = n_p + ts
    h_all = _router(xp, mods_p[0], mods_p[1], g2, w_r, b_r, None, n_tok, 0)
    h_all = _router(xs_flat, mods_s[0], mods_s[1], g2, w_r, b_r, h_all, n_tok, n_p)
    bucket = h_all[:, d].astype(jnp.int32)
    tok_of_slot, tile_ea, tile_eb, tile_nval = _schedule(bucket, n_tok, MXU_ROWS)
    y_tok = _experts(h_all, tok_of_slot, tile_ea, tile_eb, tile_nval, w1_bf, w3_bf, w2_bf, layer, n_tok)
    return _combine(xp, y_tok, mods_p[2], 0), _combine(xs_flat, y_tok, mods_s[2], n_p)


def _rope_tables(pos):
    half = HEAD_DIM // 2
    inv = ROPE_THETA ** (-jnp.arange(half, dtype=F32) / half)
    ang = pos.astype(F32)[:, None] * inv[None, :]
    cos, sin = jnp.cos(ang), jnp.sin(ang)
    return jnp.concatenate([cos, cos], axis=-1), jnp.concatenate([-sin, sin], axis=-1)


def kernel(x_prompt, x_sample, state_pool, cache_k, cache_v, page_table, c_prompt, c_sample, w_ada, b_ada, g_norm1, g_norm2, w_pool, b_pool, pool_scale, w_qkv, g_q, g_k, lam_q1, lam_k1, lam_q2, lam_k2, g_sub, w_o, w_rg, b_rg, w_re, b_re, w1, w3, w2):
    bp, seq, d = x_prompt.shape
    bd, dec_t, _ = x_sample.shape
    depth = w_ada.shape[0]
    n_heads = cache_k.shape[3]
    qk_width = n_heads * 2 * HEAD_DIM
    page = cache_k.shape[2]
    past_len = page_table.shape[1] * page
    ts = bd * dec_t
    rows = max(8, 1 << (dec_t - 1).bit_length())
    pool_state = state_pool.shape[2]

    n_c = bp + bd
    c_all = jnp.concatenate([c_prompt, c_sample, jnp.zeros((-(-n_c // 8) * 8 - n_c, d), F32)], axis=0)
    mod = _ada(c_all, w_ada, b_ada)

    cos_p, sin_p = _rope_tables(jnp.arange(seq))
    cos_s, sin_s = _rope_tables(past_len + jnp.arange(dec_t))
    cos_s, sin_s = jnp.tile(cos_s, (bd, 1)), jnp.tile(sin_s, (bd, 1))
    cache_k2 = cache_k.reshape(cache_k.shape[0], cache_k.shape[1], page * n_heads * 2, HEAD_DIM)
    cache_v2 = cache_v.reshape(cache_v.shape[0], cache_v.shape[1], page * n_heads, V_DIM)
    n_attn = depth // 2
    k_nat_p = v_nat_p = k_nat_s = v_nat_s = None
    w1_bf, w3_bf, w2_bf = w1.astype(BF16), w3.astype(BF16), w2.astype(BF16)
    w_qkv_bf, w_o_bf = w_qkv.astype(BF16), w_o.astype(BF16)
    w_r = jnp.concatenate([w_rg, w_re, jnp.zeros((depth, d, LANES - w_rg.shape[-1] - w_re.shape[-1]), F32)], axis=-1)
    b_r = jnp.concatenate([b_rg, b_re, jnp.zeros((depth, LANES - b_rg.shape[-1] - b_re.shape[-1]), F32)], axis=-1)

    xp, xs = x_prompt, x_sample
    npool_p, npool_s = [], []
    for i in range(depth):
        chunks = [mod[i, :, c * d:(c + 1) * d] for c in range(6)]
        m_p = [c[:bp, None, :] for c in chunks]
        m_s = [c[bp:bp + bd, None, :] for c in chunks]
        m_s_rows = [jnp.repeat(c[bp:bp + bd], dec_t, axis=0)[None] for c in chunks]
        g1 = g_norm1[i][None]
        g2 = g_norm2[i][None]
        li = i // 2
        if i % 2 == 0:
            w_pool_bf = w_pool[li].astype(BF16)
            b_pool_i = b_pool[li].reshape(1, d)
            ls = pool_scale[li][None]
            xp, tail_p = _pool(xp, jnp.zeros((bp, HIST, d), F32), m_p[0], m_p[1], m_p[2], g1,
                               w_pool_bf, b_pool_i, ls, 0)
            npool_p.append(tail_p[:, HIST - pool_state:])
            hist_s = jnp.pad(state_pool[li], ((0, 0), (HIST - pool_state, 0), (0, 0)))
            xs_pad = jnp.pad(xs, ((0, 0), (0, rows - dec_t), (0, 0)))
            xs_new, tail_s = _pool(xs_pad, hist_s, m_s[0], m_s[1], m_s[2], g1, w_pool_bf, b_pool_i, ls, past_len)
            xs = xs_new[:, :dec_t]
            ext = jnp.concatenate([hist_s[:, :rows], tail_s], axis=1)
            npool_s.append(ext[:, HIST + dec_t - pool_state:HIST + dec_t])
        else:
            gq, gk, gs = g_q[li][None], g_k[li][None], g_sub[li][None]
            lam_vecs = jnp.stack([lam_q1[li], lam_k1[li], lam_q2[li], lam_k2[li]])
            q, k_bf, vt_bf, k_nat_p, v_nat_p = _qkv(xp, m_p[0], m_p[1], g1, w_qkv_bf, gq, gk, cos_p, sin_p, qk_width,
                                                    n_attn, li, k_nat_p, v_nat_p, True)
            o = _prompt_attn(q, k_bf, vt_bf, lam_vecs, g_sub[li][:, None], i)
            xp = _oproj(o, w_o_bf, li, xp, m_p[2])
            xs_flat = xs.reshape(1, ts, d)
            q, k_bf, v_bf, k_nat_s, v_nat_s = _qkv(xs_flat, m_s_rows[0], m_s_rows[1], g1, w_qkv_bf, gq, gk,
                                                   cos_s, sin_s, qk_width, n_attn, li, k_nat_s, v_nat_s, False)
            o = _decode_attn(q.reshape(bd, dec_t, n_heads, 2, HEAD_DIM), k_bf.reshape(bd, dec_t, n_heads, 2, HEAD_DIM),
                             v_bf.reshape(bd, dec_t, n_heads, V_DIM), cache_k2, cache_v2,
                             page_table, li, lam_vecs, gs, i, rows)
            xs = _oproj(o.reshape(1, ts, -1), w_o_bf, li, xs_flat, m_s_rows[2]).reshape(bd, dec_t, d)
        xp, xs_flat = _moe(xp, xs.reshape(1, ts, d), m_p[3:], m_s_rows[3:], g2, w_r[i], b_r[i][None],
                           w1_bf, w3_bf, w2_bf, i)
        xs = xs_flat.reshape(bd, dec_t, d)
    def v_out(v_nat, b, s):
        v = v_nat.reshape(n_attn, b, s, V_DIM // LANES, n_heads, LANES)
        return v.transpose(0, 1, 2, 4, 3, 5).reshape(n_attn, b, s, n_heads, V_DIM)

    return (xp, xs,
            k_nat_p.reshape(n_attn, bp, seq, n_heads, 2, HEAD_DIM), v_out(v_nat_p, bp, seq),
            jnp.stack(npool_p),
            k_nat_s.reshape(n_attn, bd, dec_t, n_heads, 2, HEAD_DIM), v_out(v_nat_s, bd, dec_t),
            jnp.stack(npool_s))
```
